```python
import jax, jax.numpy as jnp
from jax import lax
import numpy as np

D_MODEL = 1024
BATCH = 32
SEQ = 2048
DEPTH = 1

POOL_WIDTH = D_MODEL // 2
POOL_GROUPS = 4
POOL_GROUP_DIM = POOL_WIDTH // POOL_GROUPS
POOL_WINDOWS = (2, 4, 8, 16)
N_HEADS = 8
HEAD_DIM = 64
ATTN_WIDTH = N_HEADS * HEAD_DIM
KV_DIM = HEAD_DIM
IDX_HEADS = 8
IDX_DIM = 64
TOPK_MAX = 256
ROPE_THETA = 500000.0
ROT_DIM = HEAD_DIM // 4
Q_BLOCK = 128
EPS = 1e-6

IN_SIZES = (
    POOL_WIDTH,
    POOL_WIDTH,
    ATTN_WIDTH,
    KV_DIM,
    KV_DIM,
    ATTN_WIDTH,
    IDX_HEADS * IDX_DIM,
    IDX_DIM,
    IDX_HEADS,
    D_MODEL,
    D_MODEL,
)
IN_WIDTH = 2 * POOL_WIDTH + 2 * ATTN_WIDTH + 2 * KV_DIM + IDX_HEADS * IDX_DIM + IDX_DIM + IDX_HEADS + 2 * D_MODEL

kernel_name = "hybrid_pool_dsa_gated_block"


def rmsnorm(x, gain):
    xf = x.astype(jnp.float32)
    out = xf * lax.rsqrt(jnp.mean(xf * xf, axis=-1, keepdims=True) + EPS)
    return (out * gain.astype(jnp.float32)).astype(x.dtype)


def rope_tables(positions):
    inv_freq = ROPE_THETA ** (-jnp.arange(0, ROT_DIM, 2, dtype=jnp.float32) / ROT_DIM)
    ang = positions.astype(jnp.float32)[..., None] * inv_freq
    return jnp.cos(ang)[:, :, None, :], jnp.sin(ang)[:, :, None, :]


def apply_partial_rope(x, cos, sin):
    half = ROT_DIM // 2
    x1 = x[..., :half].astype(jnp.float32)
    x2 = x[..., half:ROT_DIM].astype(jnp.float32)
    rot = jnp.concatenate([x1 * cos - x2 * sin, x2 * cos + x1 * sin], axis=-1)
    return jnp.concatenate([rot.astype(x.dtype), x[..., ROT_DIM:]], axis=-1)


def multiscale_pool(u, w_group, scale):
    B, L, _ = u.shape
    ug = u.reshape(B, L, POOL_GROUPS, POOL_GROUP_DIM).astype(jnp.float32)
    cs = jnp.cumsum(ug, axis=1)
    cs_pad = jnp.concatenate([jnp.zeros_like(cs[:, :1]), cs], axis=1)
    t = jnp.arange(L)
    groups = []
    for g, w in enumerate(POOL_WINDOWS):
        start = jnp.maximum(t + 1 - w, 0)
        win_sum = cs[:, :, g] - cs_pad[:, start, g]
        count = (t + 1 - start).astype(jnp.float32)[None, :, None]
        groups.append(win_sum / count - ug[:, :, g])
    pooled = jnp.stack(groups, axis=2).astype(u.dtype)
    mixed = jnp.einsum('blgc,gcd->blgd', pooled, w_group)
    return mixed.reshape(B, L, POOL_WIDTH) * scale


def dsa_sparse_attention(q, k, v, q_idx, k_idx, w_idx):
    B, L = q.shape[0], q.shape[1]
    n_blk = L // Q_BLOCK
    top_k = min(TOPK_MAX, L // 4)
    attn_scale = HEAD_DIM ** -0.5
    key_pos = jnp.arange(L)
    k_idx_f = k_idx.astype(jnp.float32)

    def to_blocks(a):
        return a.reshape((B, n_blk, Q_BLOCK) + a.shape[2:]).swapaxes(0, 1)

    def block(args):
        qb, qib, wb, blk = args
        tq = blk * Q_BLOCK + jnp.arange(Q_BLOCK)
        s = jax.nn.relu(jnp.einsum('bqhd,bsd->bqhs', qib.astype(jnp.float32), k_idx_f))
        score = jnp.einsum('bqhs,bqh->bqs', s, wb.astype(jnp.float32))
        causal = key_pos[None, :] <= tq[:, None]
        score = jnp.where(causal[None], score, -jnp.inf)
        _, idx = lax.top_k(score, top_k)
        k_sel = jax.vmap(lambda kb, ib: kb[ib])(k, idx)
        v_sel = jax.vmap(lambda vb, ib: vb[ib])(v, idx)
        logits = jnp.einsum('bqhd,bqkd->bqhk', qb, k_sel).astype(jnp.float32) * attn_scale
        valid = idx <= tq[None, :, None]
        logits = jnp.where(valid[:, :, None, :], logits, -jnp.inf)
        p = jax.nn.softmax(logits, axis=-1).astype(v.dtype)
        return jnp.einsum('bqhk,bqkd->bqhd', p, v_sel)

    out = lax.map(block, (to_blocks(q), to_blocks(q_idx), to_blocks(w_idx), jnp.arange(n_blk)))
    return out.swapaxes(0, 1).reshape(B, L, N_HEADS * HEAD_DIM)


def setup_inputs(seed: int = 0) -> dict:
    key = jax.random.key(seed)
    ks = jax.random.split(key, 16)
    f32 = jnp.float32
    x = jax.random.normal(ks[0], (BATCH, SEQ, D_MODEL), f32)
    c = jax.random.normal(ks[1], (BATCH, D_MODEL), f32)
    offsets = jax.random.randint(ks[2], (BATCH, 1), 0, 1024, dtype=jnp.int32)
    positions = offsets + jnp.arange(SEQ, dtype=jnp.int32)[None, :]
    w_ada = jax.random.normal(ks[3], (DEPTH, D_MODEL, 3 * D_MODEL), f32) * (0.5 * D_MODEL ** -0.5)
    b_ada = jax.random.normal(ks[4], (DEPTH, 3 * D_MODEL), f32) * 0.02
    norm_gain = 1.0 + 0.02 * jax.random.normal(ks[5], (DEPTH, D_MODEL), f32)
    w_in = jax.random.normal(ks[6], (DEPTH, D_MODEL, IN_WIDTH), f32) * D_MODEL ** -0.5
    w_pool_group = jax.random.normal(ks[7], (DEPTH, POOL_GROUPS, POOL_GROUP_DIM, POOL_GROUP_DIM), f32) * POOL_GROUP_DIM ** -0.5
    pool_scale = 1.0 + 0.02 * jax.random.normal(ks[8], (DEPTH, POOL_WIDTH), f32)
    q_norm_gain = 1.0 + 0.02 * jax.random.normal(ks[9], (DEPTH, HEAD_DIM), f32)
    k_norm_gain = 1.0 + 0.02 * jax.random.normal(ks[10], (DEPTH, HEAD_DIM), f32)
    w_branch_a = jax.random.normal(ks[11], (DEPTH, POOL_WIDTH, D_MODEL), f32) * POOL_WIDTH ** -0.5
    w_branch_b = jax.random.normal(ks[12], (DEPTH, ATTN_WIDTH, D_MODEL), f32) * ATTN_WIDTH ** -0.5
    w_out = jax.random.normal(ks[13], (DEPTH, D_MODEL, D_MODEL), f32) * D_MODEL ** -0.5
    return {"x": x, "c": c, "positions": positions, "w_ada": w_ada, "b_ada": b_ada,
            "norm_gain": norm_gain, "w_in": w_in, "w_pool_group": w_pool_group,
            "pool_scale": pool_scale, "q_norm_gain": q_norm_gain, "k_norm_gain": k_norm_gain,
            "w_branch_a": w_branch_a, "w_branch_b": w_branch_b, "w_out": w_out}


def reference(x, c, positions, w_ada, b_ada, norm_gain, w_in, w_pool_group, pool_scale,
              q_norm_gain, k_norm_gain, w_branch_a, w_branch_b, w_out):
    B, L, _ = x.shape
    cos, sin = rope_tables(positions)
    split_points = tuple(int(s) for s in np.cumsum(IN_SIZES)[:-1])
    idx_w_scale = (IDX_HEADS ** -0.5) * (IDX_DIM ** -0.5)
    for layer in range(DEPTH):
        mod = jax.nn.silu(c) @ w_ada[layer] + b_ada[layer]
        shift, scale, gate = jnp.split(mod, 3, axis=-1)
        h = rmsnorm(x, norm_gain[layer]) * (1.0 + scale[:, None, :]) + shift[:, None, :]

        proj = h @ w_in[layer]
        (u_a, z_a, q, k, v, z_b, q_idx, k_idx, w_idx, g_a, g_b) = jnp.split(proj, split_points, axis=-1)

        y_a = multiscale_pool(u_a, w_pool_group[layer], pool_scale[layer]) * jax.nn.silu(z_a)

        q = apply_partial_rope(rmsnorm(q.reshape(B, L, N_HEADS, HEAD_DIM), q_norm_gain[layer]), cos, sin)
        k = apply_partial_rope(rmsnorm(k.reshape(B, L, 1, KV_DIM), k_norm_gain[layer]), cos, sin)[:, :, 0]
        q_idx = apply_partial_rope(q_idx.reshape(B, L, IDX_HEADS, IDX_DIM), cos, sin)
        k_idx = apply_partial_rope(k_idx.reshape(B, L, 1, IDX_DIM), cos, sin)[:, :, 0]
        attn = dsa_sparse_attention(q, k, v, q_idx, k_idx, w_idx * idx_w_scale)
        y_b = attn * jax.nn.silu(z_b)

        merged = jax.nn.sigmoid(g_a) * (y_a @ w_branch_a[layer]) + jax.nn.sigmoid(g_b) * (y_b @ w_branch_b[layer])
        x = x + gate[:, None, :] * (merged @ w_out[layer])
    return x
```

```python
import functools

import jax
import jax.numpy as jnp
from jax import lax
from jax.experimental import pallas as pl
from jax.experimental.pallas import tpu as pltpu

D_MODEL = 1024
POOL_WIDTH = 512
POOL_GROUPS = 4
POOL_GROUP_DIM = 128
POOL_WINDOWS = (2, 4, 8, 16)
MAX_WINDOW = 16
N_HEADS = 8
HEAD_DIM = 64
ATTN_WIDTH = N_HEADS * HEAD_DIM
IDX_HEADS = 8
IDX_DIM = 64
TOPK_MAX = 256
ROPE_THETA = 500000.0
ROT_DIM = HEAD_DIM // 4
ROT_HALF = ROT_DIM // 2
EPS = 1e-6
IN_SIZES = (POOL_WIDTH, POOL_WIDTH, ATTN_WIDTH, HEAD_DIM, HEAD_DIM, ATTN_WIDTH,
            IDX_HEADS * IDX_DIM, IDX_DIM, IDX_HEADS, D_MODEL, D_MODEL)

TILE = 256
CHUNK = 256
V_ROWS = 80
NEG_BIG = -1e30
VMEM_LIMIT_BYTES = 56 * 1024 * 1024

T_Q, T_QI, T_K, T_KI, T_V, T_W, T_END = 0, 512, 1024, 1088, 1152, 1216, 1232
N_UA, N_ZA, N_ZB, N_GA, N_GB, N_END = 0, 512, 1024, 1536, 2560, 3584

_NT_DIMS = (((1,), (1,)), ((), ()))


def _mod_kernel(c_ref, w_ref, b_ref, o_ref):
    c = c_ref[...]
    s = (c * jax.nn.sigmoid(c)).astype(jnp.bfloat16)
    o_ref[...] = jnp.dot(s, w_ref[...].astype(jnp.bfloat16),
                         preferred_element_type=jnp.float32) + b_ref[...]


def _rope_rows(blk, cos, sin):
    x1 = blk[0:ROT_HALF]
    x2 = blk[ROT_HALF:ROT_DIM]
    return jnp.concatenate([x1 * cos - x2 * sin, x2 * cos + x1 * sin, blk[ROT_DIM:]], axis=0)


def _rms_rows(blk, gain_col):
    ms = jnp.mean(blk * blk, axis=0, keepdims=True)
    return blk * lax.rsqrt(ms + EPS) * gain_col


def _layer_kernel(x_ref, mod_ref, pos_ref, gain_ref, invf_ref, wn_ref, wt_ref, wpool_ref,
                  pscale_ref, qg_ref, kg_ref, wa_ref, wb_ref, wo_ref, out_ref,
                  ext_ref, k_ref, ki_ref, vt_ref, qt_ref, qit_ref, widx_ref, key_ref, bias_ref,
                  m_ref, acc_ref, attn_ref, *, seq_len, top_k):
    i = pl.program_id(1)
    q0 = i * TILE
    nch = (q0 + TILE) // CHUNK
    f32, bf16, i32 = jnp.float32, jnp.bfloat16, jnp.int32

    x = x_ref[0]
    shift = mod_ref[0, 0:1, :]
    scale = mod_ref[0, 1:2, :]
    gate = mod_ref[0, 2:3, :]
    ms = jnp.mean(x * x, axis=-1, keepdims=True)
    h = (x * lax.rsqrt(ms + EPS) * gain_ref[...]) * (1.0 + scale) + shift
    hb = h.astype(bf16)

    pt = lax.dot_general(wt_ref[...], hb, _NT_DIMS, preferred_element_type=f32)
    ang = invf_ref[...] * pos_ref[0].astype(f32)
    cos = jnp.cos(ang)
    sin = jnp.sin(ang)
    attn_scale = HEAD_DIM ** -0.5
    for hd in range(N_HEADS):
        blk = pt[T_Q + hd * HEAD_DIM:T_Q + (hd + 1) * HEAD_DIM]
        blk = _rope_rows(_rms_rows(blk, qg_ref[...]), cos, sin) * attn_scale
        qt_ref[hd * HEAD_DIM:(hd + 1) * HEAD_DIM, :] = blk.astype(bf16)
        blk = pt[T_QI + hd * IDX_DIM:T_QI + (hd + 1) * IDX_DIM]
        qit_ref[hd * IDX_DIM:(hd + 1) * IDX_DIM, :] = _rope_rows(blk, cos, sin).astype(bf16)
    kt = _rope_rows(_rms_rows(pt[T_K:T_KI], kg_ref[...]), cos, sin)
    kit = _rope_rows(pt[T_KI:T_V], cos, sin)
    kk = jnp.concatenate([kt, kit], axis=0).T.astype(bf16)
    row = lax.broadcasted_iota(i32, (V_ROWS - HEAD_DIM, CHUNK), 0)
    ones_rows = jnp.where(row == 0, 1.0, 0.0).astype(bf16)
    for j in range(TILE // CHUNK):
        c = i * (TILE // CHUNK) + j
        k_ref[c] = kk[j * CHUNK:(j + 1) * CHUNK, 0:HEAD_DIM]
        ki_ref[c] = kk[j * CHUNK:(j + 1) * CHUNK, HEAD_DIM:2 * HEAD_DIM]
        vt_ref[c, 0:HEAD_DIM, :] = pt[T_V:T_W, j * CHUNK:(j + 1) * CHUNK].astype(bf16)
        vt_ref[c, HEAD_DIM:V_ROWS, :] = ones_rows
    idx_w_scale = (IDX_HEADS ** -0.5) * (IDX_DIM ** -0.5)
    widx_ref[...] = pt[T_W:T_W + IDX_HEADS] * idx_w_scale

    key_row = lax.broadcasted_iota(i32, (CHUNK, TILE), 0)
    q_pos = q0 + lax.broadcasted_iota(i32, (CHUNK, TILE), 1)

    def score_body(c, carry):
        kic = ki_ref[c]
        score = jnp.zeros((CHUNK, TILE), f32)
        for hd in range(IDX_HEADS):
            s = jnp.dot(kic, qit_ref[hd * IDX_DIM:(hd + 1) * IDX_DIM, :], preferred_element_type=f32)
            score = score + jnp.maximum(s, 0.0) * widx_ref[hd:hd + 1, :]
        causal = (c * CHUNK + key_row) <= q_pos
        score = jnp.where(causal, score, -jnp.inf)
        score = jnp.where(score == 0.0, 0.0, score)
        bits = pltpu.bitcast(score, i32)
        key_ref[c] = bits ^ ((bits >> 31) & jnp.int32(0x7FFFFFFF))
        return carry

    lax.fori_loop(0, nch, score_body, 0)

    def count_ge(mid):
        def body(c, acc):
            one = jnp.where(key_ref[c] >= mid, 1, 0).astype(i32)
            return acc + jnp.sum(one.reshape(CHUNK // 32, 32, TILE), axis=0)
        acc = lax.fori_loop(0, nch, body, jnp.zeros((32, TILE), i32))
        return jnp.sum(acc, axis=0, keepdims=True)

    def bisect_body(_, carry):
        lo, hi, cnt_hi = carry
        mid = (lo >> 1) + (hi >> 1) + (lo & hi & 1)
        cnt = count_ge(mid)
        ge = cnt >= top_k
        return jnp.where(ge, mid, lo), jnp.where(ge, hi, mid), jnp.where(ge, cnt_hi, cnt)

    int_min = jnp.iinfo(jnp.int32).min
    int_max = jnp.iinfo(jnp.int32).max
    lo0 = jnp.full((1, TILE), int_min, i32)
    hi0 = jnp.full((1, TILE), int_max, i32)
    thr, _, cnt_gt = lax.fori_loop(0, 32, bisect_body, (lo0, hi0, jnp.zeros((1, TILE), i32)))
    ties_wanted = (top_k - cnt_gt).astype(f32)

    lower_tri = (lax.broadcasted_iota(i32, (CHUNK, CHUNK), 1)
                 <= lax.broadcasted_iota(i32, (CHUNK, CHUNK), 0))
    lower_tri = jnp.where(lower_tri, 1.0, 0.0).astype(bf16)

    def select_body(c, seen):
        key = key_ref[c]
        tie = key == thr
        tie_f = jnp.where(tie, 1.0, 0.0)
        rank = jnp.dot(lower_tri, tie_f.astype(bf16), preferred_element_type=f32) + seen
        sel = (key > thr) | (tie & (rank <= ties_wanted))
        causal = (c * CHUNK + key_row) <= q_pos
        bias_ref[c] = jnp.where(sel & causal, 0.0, -jnp.inf)
        return seen + jnp.sum(tie_f, axis=0, keepdims=True)

    lax.fori_loop(0, nch, select_body, jnp.zeros((1, TILE), f32))

    m_ref[...] = jnp.full((N_HEADS, TILE), NEG_BIG, f32)
    acc_ref[...] = jnp.zeros((N_HEADS, V_ROWS, TILE), f32)

    def attn_body(c, carry):
        kc = k_ref[c]
        vc = vt_ref[c]
        bias = bias_ref[c]
        for hd in range(N_HEADS):
            lt = jnp.dot(kc, qt_ref[hd * HEAD_DIM:(hd + 1) * HEAD_DIM, :],
                         preferred_element_type=f32) + bias
            m_old = m_ref[hd:hd + 1, :]
            m_new = jnp.maximum(m_old, jnp.max(lt, axis=0, keepdims=True))
            p = jnp.exp(lt - m_new).astype(bf16)
            alpha = jnp.exp(m_old - m_new)
            acc_ref[hd] = acc_ref[hd] * alpha + jnp.dot(vc, p, preferred_element_type=f32)
            m_ref[hd:hd + 1, :] = m_new
        return carry

    lax.fori_loop(0, nch, attn_body, 0)
    for hd in range(N_HEADS):
        a = acc_ref[hd]
        attn_ref[hd * HEAD_DIM:(hd + 1) * HEAD_DIM, :] = a[0:HEAD_DIM] / a[HEAD_DIM:HEAD_DIM + 1]
    attn = attn_ref[...].T

    u_a = jnp.dot(hb, wn_ref[:, N_UA:N_ZA], preferred_element_type=f32)

    @pl.when(i == 0)
    def _():
        ext_ref[0:MAX_WINDOW, :] = jnp.zeros((MAX_WINDOW, POOL_WIDTH), f32)

    @pl.when(i > 0)
    def _():
        ext_ref[0:MAX_WINDOW, :] = ext_ref[TILE:TILE + MAX_WINDOW, :]

    ext_ref[MAX_WINDOW:MAX_WINDOW + TILE, :] = u_a
    t_pos = q0 + lax.broadcasted_iota(i32, (TILE, 1), 0)
    mixed = []
    for g, w in enumerate(POOL_WINDOWS):
        cols = slice(g * POOL_GROUP_DIM, (g + 1) * POOL_GROUP_DIM)
        win = ext_ref[MAX_WINDOW:MAX_WINDOW + TILE, cols]
        for j in range(1, w):
            win = win + ext_ref[MAX_WINDOW - j:MAX_WINDOW - j + TILE, cols]
        count = jnp.minimum(t_pos + 1, w).astype(f32)
        pooled = win / count - u_a[:, cols]
        mixed.append(jnp.dot(pooled.astype(bf16), wpool_ref[g], preferred_element_type=f32))
    mixed = jnp.concatenate(mixed, axis=1) * pscale_ref[...]
    z_a = jnp.dot(hb, wn_ref[:, N_ZA:N_ZB], preferred_element_type=f32)
    y_a = mixed * (z_a * jax.nn.sigmoid(z_a))
    a_out = jnp.dot(y_a.astype(bf16), wa_ref[...], preferred_element_type=f32)

    z_b = jnp.dot(hb, wn_ref[:, N_ZB:N_GA], preferred_element_type=f32)
    y_b = attn * (z_b * jax.nn.sigmoid(z_b))
    b_out = jnp.dot(y_b.astype(bf16), wb_ref[...], preferred_element_type=f32)
    g_a = jnp.dot(hb, wn_ref[:, N_GA:N_GB], preferred_element_type=f32)
    g_b = jnp.dot(hb, wn_ref[:, N_GB:N_END], preferred_element_type=f32)
    merged = jax.nn.sigmoid(g_a) * a_out + jax.nn.sigmoid(g_b) * b_out
    y = jnp.dot(merged.astype(bf16), wo_ref[...], preferred_element_type=f32)
    out_ref[0] = x + gate * y


def _const_spec(shape):
    zeros = (0,) * len(shape)
    return pl.BlockSpec(shape, lambda b, i: zeros)


def _layer(x, mod3, pos3, gain, invf, w_n, w_t, w_pool, pscale, qg, kg, w_a, w_b, w_o):
    batch, seq_len, d = x.shape
    assert d == D_MODEL and seq_len % TILE == 0 and TILE % CHUNK == 0
    n_chunks = seq_len // CHUNK
    top_k = min(TOPK_MAX, seq_len // 4)
    bf16, f32, i32 = jnp.bfloat16, jnp.float32, jnp.int32
    kernel = functools.partial(_layer_kernel, seq_len=seq_len, top_k=top_k)
    return pl.pallas_call(
        kernel,
        grid=(batch, seq_len // TILE),
        in_specs=[
            pl.BlockSpec((1, TILE, D_MODEL), lambda b, i: (b, i, 0)),
            pl.BlockSpec((1, 3, D_MODEL), lambda b, i: (b, 0, 0)),
            pl.BlockSpec((1, 1, TILE), lambda b, i: (b, 0, i)),
            _const_spec(gain.shape), _const_spec(invf.shape), _const_spec(w_n.shape),
            _const_spec(w_t.shape), _const_spec(w_pool.shape), _const_spec(pscale.shape),
            _const_spec(qg.shape), _const_spec(kg.shape), _const_spec(w_a.shape),
            _const_spec(w_b.shape), _const_spec(w_o.shape),
        ],
        out_specs=pl.BlockSpec((1, TILE, D_MODEL), lambda b, i: (b, i, 0)),
        out_shape=jax.ShapeDtypeStruct(x.shape, x.dtype),
        scratch_shapes=[
            pltpu.VMEM((TILE + MAX_WINDOW, POOL_WIDTH), f32),
            pltpu.VMEM((n_chunks, CHUNK, HEAD_DIM), bf16),
            pltpu.VMEM((n_chunks, CHUNK, IDX_DIM), bf16),
            pltpu.VMEM((n_chunks, V_ROWS, CHUNK), bf16),
            pltpu.VMEM((ATTN_WIDTH, TILE), bf16),
            pltpu.VMEM((IDX_HEADS * IDX_DIM, TILE), bf16),
            pltpu.VMEM((IDX_HEADS, TILE), f32),
            pltpu.VMEM((n_chunks, CHUNK, TILE), i32),
            pltpu.VMEM((n_chunks, CHUNK, TILE), f32),
            pltpu.VMEM((N_HEADS, TILE), f32),
            pltpu.VMEM((N_HEADS, V_ROWS, TILE), f32),
            pltpu.VMEM((ATTN_WIDTH, TILE), f32),
        ],
        compiler_params=pltpu.CompilerParams(
            dimension_semantics=("arbitrary", "arbitrary"),
            vmem_limit_bytes=VMEM_LIMIT_BYTES),
        name="hybrid_layer",
    )(x, mod3, pos3, gain, invf, w_n, w_t, w_pool, pscale, qg, kg, w_a, w_b, w_o)


def _modulation(c, w_ada, b_ada):
    batch, d = c.shape
    n_out = w_ada.shape[1]
    blk = D_MODEL
    return pl.pallas_call(
        _mod_kernel,
        grid=(n_out // blk,),
        in_specs=[pl.BlockSpec((batch, d), lambda j: (0, 0)),
                  pl.BlockSpec((d, blk), lambda j: (0, j)),
                  pl.BlockSpec((1, blk), lambda j: (0, j))],
        out_specs=pl.BlockSpec((batch, blk), lambda j: (0, j)),
        out_shape=jax.ShapeDtypeStruct((batch, n_out), jnp.float32),
        name="adaln_mod",
    )(c, w_ada, b_ada.reshape(1, n_out))


def kernel(x, c, positions, w_ada, b_ada, norm_gain, w_in, w_pool_group, pool_scale,
           q_norm_gain, k_norm_gain, w_branch_a, w_branch_b, w_out):
    batch, seq_len, _ = x.shape
    bf16 = jnp.bfloat16
    inv_freq = ROPE_THETA ** (-jnp.arange(0, ROT_DIM, 2, dtype=jnp.float32) / ROT_DIM)
    invf = inv_freq.reshape(ROT_HALF, 1)
    pos3 = positions.reshape(batch, 1, seq_len)
    bounds = [0]
    for s in IN_SIZES:
        bounds.append(bounds[-1] + s)
    depth = w_in.shape[0]
    for layer in range(depth):
        seg = [w_in[layer][:, bounds[j]:bounds[j + 1]] for j in range(len(IN_SIZES))]
        u_a, z_a, q, k, v, z_b, q_idx, k_idx, w_idx, g_a, g_b = seg
        w_n = jnp.concatenate([u_a, z_a, z_b, g_a, g_b], axis=1).astype(bf16)
        pad = jnp.zeros((D_MODEL, T_END - T_W - IDX_HEADS), jnp.float32)
        w_t = jnp.concatenate([q, q_idx, k, k_idx, v, w_idx, pad], axis=1).T.astype(bf16)
        mod = _modulation(c, w_ada[layer], b_ada[layer])
        x = _layer(
            x, mod.reshape(batch, 3, D_MODEL), pos3, norm_gain[layer].reshape(1, D_MODEL), invf,
            w_n, w_t, w_pool_group[layer].astype(bf16), pool_scale[layer].reshape(1, POOL_WIDTH),
            q_norm_gain[layer].reshape(HEAD_DIM, 1), k_norm_gain[layer].reshape(HEAD_DIM, 1),
            w_branch_a[layer].astype(bf16), w_branch_b[layer].astype(bf16), w_out[layer].astype(bf16))
    return x
```

```python
import functools

import jax
import jax.numpy as jnp
from jax import lax
from jax.experimental import pallas as pl
from jax.experimental.pallas import tpu as pltpu

D_MODEL = 1024
POOL_WIDTH = 512
POOL_GROUPS = 4
POOL_GROUP_DIM = 128
POOL_WINDOWS = (2, 4, 8, 16)
MAX_WINDOW = 16
N_HEADS = 8
HEAD_DIM = 64
ATTN_WIDTH = N_HEADS * HEAD_DIM
IDX_HEADS = 8
IDX_DIM = 64
TOPK_MAX = 256
ROPE_THETA = 500000.0
ROT_DIM = HEAD_DIM // 4
ROT_HALF = ROT_DIM // 2
EPS = 1e-6
IN_SIZES = (POOL_WIDTH, POOL_WIDTH, ATTN_WIDTH, HEAD_DIM, HEAD_DIM, ATTN_WIDTH,
            IDX_HEADS * IDX_DIM, IDX_DIM, IDX_HEADS, D_MODEL, D_MODEL)

TILE = 256
CHUNK = 256
V_ROWS = 80
NEG_BIG = -1e30
VMEM_LIMIT_BYTES = 56 * 1024 * 1024

T_Q, T_QI, T_K, T_KI, T_V, T_W, T_END = 0, 512, 1024, 1088, 1152, 1216, 1232
N_UA, N_ZA, N_ZB, N_GA, N_GB, N_END = 0, 512, 1024, 1536, 2560, 3584

_NT_DIMS = (((1,), (1,)), ((), ()))


def _mod_kernel(c_ref, w_ref, b_ref, o_ref):
    c = c_ref[...]
    s = (c * jax.nn.sigmoid(c)).astype(jnp.bfloat16)
    o_ref[...] = jnp.dot(s, w_ref[...].astype(jnp.bfloat16),
                         preferred_element_type=jnp.float32) + b_ref[...]


def _rope_rows(blk, cos, sin):
    x1 = blk[0:ROT_HALF]
    x2 = blk[ROT_HALF:ROT_DIM]
    return jnp.concatenate([x1 * cos - x2 * sin, x2 * cos + x1 * sin, blk[ROT_DIM:]], axis=0)


def _rms_rows(blk, gain_col):
    ms = jnp.mean(blk * blk, axis=0, keepdims=True)
    return blk * lax.rsqrt(ms + EPS) * gain_col


def _layer_kernel(x_ref, mod_ref, pos_ref, gain_ref, invf_ref, wn_ref, wt_ref, wpool_ref,
                  pscale_ref, qg_ref, kg_ref, wa_ref, wb_ref, wo_ref, out_ref,
                  ext_ref, k_ref, ki_ref, vt_ref, qt_ref, qit_ref, widx_ref, key_ref, bias_ref,
                  m_ref, acc_ref, attn_ref, lt_ref, *, seq_len, top_k):
    i = pl.program_id(1)
    q0 = i * TILE
    nch = (q0 + TILE) // CHUNK
    f32, bf16, i32 = jnp.float32, jnp.bfloat16, jnp.int32

    x = x_ref[0]
    shift = mod_ref[0, 0:1, :]
    scale = mod_ref[0, 1:2, :]
    gate = mod_ref[0, 2:3, :]
    ms = jnp.mean(x * x, axis=-1, keepdims=True)
    h = (x * lax.rsqrt(ms + EPS) * gain_ref[...]) * (1.0 + scale) + shift
    hb = h.astype(bf16)

    pt = lax.dot_general(wt_ref[...], hb, _NT_DIMS, preferred_element_type=f32)
    ang = invf_ref[...] * pos_ref[0].astype(f32)
    cos = jnp.cos(ang)
    sin = jnp.sin(ang)
    attn_scale = HEAD_DIM ** -0.5
    for hd in range(N_HEADS):
        blk = pt[T_Q + hd * HEAD_DIM:T_Q + (hd + 1) * HEAD_DIM]
        blk = _rope_rows(_rms_rows(blk, qg_ref[...]), cos, sin) * attn_scale
        qt_ref[hd * HEAD_DIM:(hd + 1) * HEAD_DIM, :] = blk.astype(bf16)
        blk = pt[T_QI + hd * IDX_DIM:T_QI + (hd + 1) * IDX_DIM]
        qit_ref[hd * IDX_DIM:(hd + 1) * IDX_DIM, :] = _rope_rows(blk, cos, sin).astype(bf16)
    kt = _rope_rows(_rms_rows(pt[T_K:T_KI], kg_ref[...]), cos, sin)
    kit = _rope_rows(pt[T_KI:T_V], cos, sin)
    kk = jnp.concatenate([kt, kit], axis=0).T.astype(bf16)
    row = lax.broadcasted_iota(i32, (V_ROWS - HEAD_DIM, CHUNK), 0)
    ones_rows = jnp.where(row == 0, 1.0, 0.0).astype(bf16)
    for j in range(TILE // CHUNK):
        c = i * (TILE // CHUNK) + j
        k_ref[c] = kk[j * CHUNK:(j + 1) * CHUNK, 0:HEAD_DIM]
        ki_ref[c] = kk[j * CHUNK:(j + 1) * CHUNK, HEAD_DIM:2 * HEAD_DIM]
        vt_ref[c, 0:HEAD_DIM, :] = pt[T_V:T_W, j * CHUNK:(j + 1) * CHUNK].astype(bf16)
        vt_ref[c, HEAD_DIM:V_ROWS, :] = ones_rows
    idx_w_scale = (IDX_HEADS ** -0.5) * (IDX_DIM ** -0.5)
    widx_ref[...] = pt[T_W:T_W + IDX_HEADS] * idx_w_scale

    key_row = lax.broadcasted_iota(i32, (CHUNK, TILE), 0)
    q_pos = q0 + lax.broadcasted_iota(i32, (CHUNK, TILE), 1)

    def score_body(c, carry):
        kic = ki_ref[c]
        score = jnp.zeros((CHUNK, TILE), f32)
        for hd in range(IDX_HEADS):
            s = jnp.dot(kic, qit_ref[hd * IDX_DIM:(hd + 1) * IDX_DIM, :], preferred_element_type=f32)
            score = score + jnp.maximum(s, 0.0) * widx_ref[hd:hd + 1, :]
        causal = (c * CHUNK + key_row) <= q_pos
        score = jnp.where(causal, score, -jnp.inf)
        score = jnp.where(score == 0.0, 0.0, score)
        bits = pltpu.bitcast(score, i32)
        key_ref[c] = bits ^ ((bits >> 31) & jnp.int32(0x7FFFFFFF))
        return carry

    lax.fori_loop(0, nch, score_body, 0)

    def count_ge(mid):
        def body(c, acc):
            one = jnp.where(key_ref[c] >= mid, 1, 0).astype(i32)
            return acc + jnp.sum(one.reshape(CHUNK // 32, 32, TILE), axis=0)
        acc = lax.fori_loop(0, nch, body, jnp.zeros((32, TILE), i32))
        return jnp.sum(acc, axis=0, keepdims=True)

    def bisect_body(_, carry):
        lo, hi, cnt_hi = carry
        mid = (lo >> 1) + (hi >> 1) + (lo & hi & 1)
        cnt = count_ge(mid)
        ge = cnt >= top_k
        return jnp.where(ge, mid, lo), jnp.where(ge, hi, mid), jnp.where(ge, cnt_hi, cnt)

    int_min = jnp.iinfo(jnp.int32).min
    int_max = jnp.iinfo(jnp.int32).max
    lo0 = jnp.full((1, TILE), int_min, i32)
    hi0 = jnp.full((1, TILE), int_max, i32)
    thr, _, cnt_gt = lax.fori_loop(0, 32, bisect_body, (lo0, hi0, jnp.zeros((1, TILE), i32)))
    ties_wanted = (top_k - cnt_gt).astype(f32)

    lower_tri = (lax.broadcasted_iota(i32, (CHUNK, CHUNK), 1)
                 <= lax.broadcasted_iota(i32, (CHUNK, CHUNK), 0))
    lower_tri = jnp.where(lower_tri, 1.0, 0.0).astype(bf16)

    def select_body(c, seen):
        key = key_ref[c]
        tie = key == thr
        tie_f = jnp.where(tie, 1.0, 0.0)
        rank = jnp.dot(lower_tri, tie_f.astype(bf16), preferred_element_type=f32) + seen
        sel = (key > thr) | (tie & (rank <= ties_wanted))
        causal = (c * CHUNK + key_row) <= q_pos
        bias_ref[c] = jnp.where(sel & causal, 0.0, -jnp.inf)
        return seen + jnp.sum(tie_f, axis=0, keepdims=True)

    lax.fori_loop(0, nch, select_body, jnp.zeros((1, TILE), f32))

    m_ref[...] = jnp.full((N_HEADS, TILE), NEG_BIG, f32)
    acc_ref[...] = jnp.zeros((N_HEADS, V_ROWS, TILE), f32)

    def attn_body(c, carry):
        kc = k_ref[c]
        vc = vt_ref[c]
        bias = bias_ref[c]
        col_max = []
        for hd in range(N_HEADS):
            lt = jnp.dot(kc, qt_ref[hd * HEAD_DIM:(hd + 1) * HEAD_DIM, :],
                         preferred_element_type=f32) + bias
            lt_ref[hd] = lt
            col_max.append(jnp.max(lt, axis=0, keepdims=True))
        for hd in range(N_HEADS):
            m_old = m_ref[hd:hd + 1, :]
            m_new = jnp.maximum(m_old, col_max[hd])
            p = jnp.exp(lt_ref[hd] - m_new).astype(bf16)
            alpha = jnp.exp(m_old - m_new)
            acc_ref[hd] = acc_ref[hd] * alpha + jnp.dot(vc, p, preferred_element_type=f32)
            m_ref[hd:hd + 1, :] = m_new
        return carry

    lax.fori_loop(0, nch, attn_body, 0)
    for hd in range(N_HEADS):
        a = acc_ref[hd]
        attn_ref[hd * HEAD_DIM:(hd + 1) * HEAD_DIM, :] = a[0:HEAD_DIM] / a[HEAD_DIM:HEAD_DIM + 1]
    attn = attn_ref[...].T

    u_a = jnp.dot(hb, wn_ref[:, N_UA:N_ZA], preferred_element_type=f32)

    @pl.when(i == 0)
    def _():
        ext_ref[0:MAX_WINDOW, :] = jnp.zeros((MAX_WINDOW, POOL_WIDTH), f32)

    @pl.when(i > 0)
    def _():
        ext_ref[0:MAX_WINDOW, :] = ext_ref[TILE:TILE + MAX_WINDOW, :]

    ext_ref[MAX_WINDOW:MAX_WINDOW + TILE, :] = u_a
    t_pos = q0 + lax.broadcasted_iota(i32, (TILE, 1), 0)
    mixed = []
    for g, w in enumerate(POOL_WINDOWS):
        cols = slice(g * POOL_GROUP_DIM, (g + 1) * POOL_GROUP_DIM)
        win = ext_ref[MAX_WINDOW:MAX_WINDOW + TILE, cols]
        for j in range(1, w):
            win = win + ext_ref[MAX_WINDOW - j:MAX_WINDOW - j + TILE, cols]
        count = jnp.minimum(t_pos + 1, w).astype(f32)
        pooled = win / count - u_a[:, cols]
        mixed.append(jnp.dot(pooled.astype(bf16), wpool_ref[g], preferred_element_type=f32))
    mixed = jnp.concatenate(mixed, axis=1) * pscale_ref[...]
    z_a = jnp.dot(hb, wn_ref[:, N_ZA:N_ZB], preferred_element_type=f32)
    y_a = mixed * (z_a * jax.nn.sigmoid(z_a))
    a_out = jnp.dot(y_a.astype(bf16), wa_ref[...], preferred_element_type=f32)

    z_b = jnp.dot(hb, wn_ref[:, N_ZB:N_GA], preferred_element_type=f32)
    y_b = attn * (z_b * jax.nn.sigmoid(z_b))
    b_out = jnp.dot(y_b.astype(bf16), wb_ref[...], preferred_element_type=f32)
    g_a = jnp.dot(hb, wn_ref[:, N_GA:N_GB], preferred_element_type=f32)
    g_b = jnp.dot(hb, wn_ref[:, N_GB:N_END], preferred_element_type=f32)
    merged = jax.nn.sigmoid(g_a) * a_out + jax.nn.sigmoid(g_b) * b_out
    y = jnp.dot(merged.astype(bf16), wo_ref[...], preferred_element_type=f32)
    out_ref[0] = x + gate * y


def _const_spec(shape):
    zeros = (0,) * len(shape)
    return pl.BlockSpec(shape, lambda b, i: zeros)


def _layer(x, mod3, pos3, gain, invf, w_n, w_t, w_pool, pscale, qg, kg, w_a, w_b, w_o):
    batch, seq_len, d = x.shape
    assert d == D_MODEL and seq_len % TILE == 0 and TILE % CHUNK == 0
    n_chunks = seq_len // CHUNK
    top_k = min(TOPK_MAX, seq_len // 4)
    bf16, f32, i32 = jnp.bfloat16, jnp.float32, jnp.int32
    kernel = functools.partial(_layer_kernel, seq_len=seq_len, top_k=top_k)
    return pl.pallas_call(
        kernel,
        grid=(batch, seq_len // TILE),
        in_specs=[
            pl.BlockSpec((1, TILE, D_MODEL), lambda b, i: (b, i, 0)),
            pl.BlockSpec((1, 3, D_MODEL), lambda b, i: (b, 0, 0)),
            pl.BlockSpec((1, 1, TILE), lambda b, i: (b, 0, i)),
            _const_spec(gain.shape), _const_spec(invf.shape), _const_spec(w_n.shape),
            _const_spec(w_t.shape), _const_spec(w_pool.shape), _const_spec(pscale.shape),
            _const_spec(qg.shape), _const_spec(kg.shape), _const_spec(w_a.shape),
            _const_spec(w_b.shape), _const_spec(w_o.shape),
        ],
        out_specs=pl.BlockSpec((1, TILE, D_MODEL), lambda b, i: (b, i, 0)),
        out_shape=jax.ShapeDtypeStruct(x.shape, x.dtype),
        scratch_shapes=[
            pltpu.VMEM((TILE + MAX_WINDOW, POOL_WIDTH), f32),
            pltpu.VMEM((n_chunks, CHUNK, HEAD_DIM), bf16),
            pltpu.VMEM((n_chunks, CHUNK, IDX_DIM), bf16),
            pltpu.VMEM((n_chunks, V_ROWS, CHUNK), bf16),
            pltpu.VMEM((ATTN_WIDTH, TILE), bf16),
            pltpu.VMEM((IDX_HEADS * IDX_DIM, TILE), bf16),
            pltpu.VMEM((IDX_HEADS, TILE), f32),
            pltpu.VMEM((n_chunks, CHUNK, TILE), i32),
            pltpu.VMEM((n_chunks, CHUNK, TILE), f32),
            pltpu.VMEM((N_HEADS, TILE), f32),
            pltpu.VMEM((N_HEADS, V_ROWS, TILE), f32),
            pltpu.VMEM((ATTN_WIDTH, TILE), f32),
            pltpu.VMEM((N_HEADS, CHUNK, TILE), f32),
        ],
        compiler_params=pltpu.CompilerParams(
            dimension_semantics=("arbitrary", "arbitrary"),
            vmem_limit_bytes=VMEM_LIMIT_BYTES),
        name="hybrid_layer",
    )(x, mod3, pos3, gain, invf, w_n, w_t, w_pool, pscale, qg, kg, w_a, w_b, w_o)


def _modulation(c, w_ada, b_ada):
    batch, d = c.shape
    n_out = w_ada.shape[1]
    blk = D_MODEL
    return pl.pallas_call(
        _mod_kernel,
        grid=(n_out // blk,),
        in_specs=[pl.BlockSpec((batch, d), lambda j: (0, 0)),
                  pl.BlockSpec((d, blk), lambda j: (0, j)),
                  pl.BlockSpec((1, blk), lambda j: (0, j))],
        out_specs=pl.BlockSpec((batch, blk), lambda j: (0, j)),
        out_shape=jax.ShapeDtypeStruct((batch, n_out), jnp.float32),
        name="adaln_mod",
    )(c, w_ada, b_ada.reshape(1, n_out))


def kernel(x, c, positions, w_ada, b_ada, norm_gain, w_in, w_pool_group, pool_scale,
           q_norm_gain, k_norm_gain, w_branch_a, w_branch_b, w_out):
    batch, seq_len, _ = x.shape
    bf16 = jnp.bfloat16
    inv_freq = ROPE_THETA ** (-jnp.arange(0, ROT_DIM, 2, dtype=jnp.float32) / ROT_DIM)
    invf = inv_freq.reshape(ROT_HALF, 1)
    pos3 = positions.reshape(batch, 1, seq_len)
    bounds = [0]
    for s in IN_SIZES:
        bounds.append(bounds[-1] + s)
    depth = w_in.shape[0]
    for layer in range(depth):
        seg = [w_in[layer][:, bounds[j]:bounds[j + 1]] for j in range(len(IN_SIZES))]
        u_a, z_a, q, k, v, z_b, q_idx, k_idx, w_idx, g_a, g_b = seg
        w_n = jnp.concatenate([u_a, z_a, z_b, g_a, g_b], axis=1).astype(bf16)
        pad = jnp.zeros((D_MODEL, T_END - T_W - IDX_HEADS), jnp.float32)
        w_t = jnp.concatenate([q, q_idx, k, k_idx, v, w_idx, pad], axis=1).T.astype(bf16)
        mod = _modulation(c, w_ada[layer], b_ada[layer])
        x = _layer(
            x, mod.reshape(batch, 3, D_MODEL), pos3, norm_gain[layer].reshape(1, D_MODEL), invf,
            w_n, w_t, w_pool_group[layer].astype(bf16), pool_scale[layer].reshape(1, POOL_WIDTH),
            q_norm_gain[layer].reshape(HEAD_DIM, 1), k_norm_gain[layer].reshape(HEAD_DIM, 1),
            w_branch_a[layer].astype(bf16), w_branch_b[layer].astype(bf16), w_out[layer].astype(bf16))
    return x
```

```python
import functools

import jax
import jax.numpy as jnp
from jax import lax
from jax.experimental import pallas as pl
from jax.experimental.pallas import tpu as pltpu

D_MODEL = 1024
POOL_WIDTH = 512
POOL_GROUPS = 4
POOL_GROUP_DIM = 128
POOL_WINDOWS = (2, 4, 8, 16)
MAX_WINDOW = 16
N_HEADS = 8
HEAD_DIM = 64
ATTN_WIDTH = N_HEADS * HEAD_DIM
IDX_HEADS = 8
IDX_DIM = 64
TOPK_MAX = 256
ROPE_THETA = 500000.0
ROT_DIM = HEAD_DIM // 4
ROT_HALF = ROT_DIM // 2
EPS = 1e-6
IN_SIZES = (POOL_WIDTH, POOL_WIDTH, ATTN_WIDTH, HEAD_DIM, HEAD_DIM, ATTN_WIDTH,
            IDX_HEADS * IDX_DIM, IDX_DIM, IDX_HEADS, D_MODEL, D_MODEL)

TILE = 256
CHUNK = 256
V_ROWS = 80
NEG_BIG = -1e30
LOG2_E = 1.4426950408889634
HALF_BITS = 16
HALF_MIN, HALF_MAX = -(1 << 15), (1 << 15) - 1
VMEM_LIMIT_BYTES = 56 * 1024 * 1024

T_Q, T_QI, T_K, T_KI, T_V, T_W, T_END = 0, 512, 1024, 1088, 1152, 1216, 1232
N_UA, N_ZA, N_ZB, N_GA, N_GB, N_END = 0, 512, 1024, 1536, 2560, 3584

_NT_DIMS = (((1,), (1,)), ((), ()))


def _mod_kernel(c_ref, w_ref, b_ref, o_ref):
    c = c_ref[...]
    s = (c * jax.nn.sigmoid(c)).astype(jnp.bfloat16)
    o_ref[...] = jnp.dot(s, w_ref[...].astype(jnp.bfloat16),
                         preferred_element_type=jnp.float32) + b_ref[...]


def _rope_rows(blk, cos, sin):
    x1 = blk[0:ROT_HALF]
    x2 = blk[ROT_HALF:ROT_DIM]
    return jnp.concatenate([x1 * cos - x2 * sin, x2 * cos + x1 * sin, blk[ROT_DIM:]], axis=0)


def _rms_rows(blk, gain_col):
    ms = jnp.mean(blk * blk, axis=0, keepdims=True)
    return blk * lax.rsqrt(ms + EPS) * gain_col


def _layer_kernel(x_ref, mod_ref, pos_ref, gain_ref, invf_ref, wn_ref, wt_ref, wpool_ref,
                  pscale_ref, qg_ref, kg_ref, wa_ref, wb_ref, wo_ref, out_ref,
                  ext_ref, k_ref, ki_ref, vt_ref, qt_ref, qit_ref, widx_ref, key_ref, bias_ref,
                  m_ref, acc_ref, attn_ref, lt_ref, half_ref, *, seq_len, top_k):
    i = pl.program_id(1)
    q0 = i * TILE
    nch = (q0 + TILE) // CHUNK
    f32, bf16, i32, i16 = jnp.float32, jnp.bfloat16, jnp.int32, jnp.int16

    x = x_ref[0]
    shift = mod_ref[0, 0:1, :]
    scale = mod_ref[0, 1:2, :]
    gate = mod_ref[0, 2:3, :]
    ms = jnp.mean(x * x, axis=-1, keepdims=True)
    h = (x * lax.rsqrt(ms + EPS) * gain_ref[...]) * (1.0 + scale) + shift
    hb = h.astype(bf16)

    pt = lax.dot_general(wt_ref[...], hb, _NT_DIMS, preferred_element_type=f32)
    ang = invf_ref[...] * pos_ref[0].astype(f32)
    cos = jnp.cos(ang)
    sin = jnp.sin(ang)
    attn_scale = HEAD_DIM ** -0.5 * LOG2_E
    for hd in range(N_HEADS):
        blk = pt[T_Q + hd * HEAD_DIM:T_Q + (hd + 1) * HEAD_DIM]
        blk = _rope_rows(_rms_rows(blk, qg_ref[...]), cos, sin) * attn_scale
        qt_ref[hd * HEAD_DIM:(hd + 1) * HEAD_DIM, :] = blk.astype(bf16)
        blk = pt[T_QI + hd * IDX_DIM:T_QI + (hd + 1) * IDX_DIM]
        qit_ref[hd * IDX_DIM:(hd + 1) * IDX_DIM, :] = _rope_rows(blk, cos, sin).astype(bf16)
    kt = _rope_rows(_rms_rows(pt[T_K:T_KI], kg_ref[...]), cos, sin)
    kit = _rope_rows(pt[T_KI:T_V], cos, sin)
    kk = jnp.concatenate([kt, kit], axis=0).T.astype(bf16)
    row = lax.broadcasted_iota(i32, (V_ROWS - HEAD_DIM, CHUNK), 0)
    ones_rows = jnp.where(row == 0, 1.0, 0.0).astype(bf16)
    for j in range(TILE // CHUNK):
        c = i * (TILE // CHUNK) + j
        k_ref[c] = kk[j * CHUNK:(j + 1) * CHUNK, 0:HEAD_DIM]
        ki_ref[c] = kk[j * CHUNK:(j + 1) * CHUNK, HEAD_DIM:2 * HEAD_DIM]
        vt_ref[c, 0:HEAD_DIM, :] = pt[T_V:T_W, j * CHUNK:(j + 1) * CHUNK].astype(bf16)
        vt_ref[c, HEAD_DIM:V_ROWS, :] = ones_rows
    idx_w_scale = (IDX_HEADS ** -0.5) * (IDX_DIM ** -0.5)
    widx_ref[...] = pt[T_W:T_W + IDX_HEADS] * idx_w_scale

    key_row = lax.broadcasted_iota(i32, (CHUNK, TILE), 0)
    q_pos = q0 + lax.broadcasted_iota(i32, (CHUNK, TILE), 1)

    def score_body(c, carry):
        kic = ki_ref[c]
        score = jnp.zeros((CHUNK, TILE), f32)
        for hd in range(IDX_HEADS):
            s = jnp.dot(kic, qit_ref[hd * IDX_DIM:(hd + 1) * IDX_DIM, :], preferred_element_type=f32)
            score = score + jnp.maximum(s, 0.0) * widx_ref[hd:hd + 1, :]
        causal = (c * CHUNK + key_row) <= q_pos
        score = jnp.where(causal, score, -jnp.inf)
        score = jnp.where(score == 0.0, 0.0, score)
        bits = pltpu.bitcast(score, i32)
        key = bits ^ ((bits >> 31) & jnp.int32(0x7FFFFFFF))
        key_ref[c] = key
        half_ref[c] = (key >> HALF_BITS).astype(i16)
        return carry

    lax.fori_loop(0, nch, score_body, 0)

    def count_ge(mid):
        mid16 = mid.astype(i16)

        def body(c, acc):
            one = jnp.where(half_ref[c] >= mid16, jnp.bfloat16(1), jnp.bfloat16(0))
            parts = [one[s * 32:(s + 1) * 32] for s in range(CHUNK // 32)]
            while len(parts) > 1:
                parts = [parts[j] + parts[j + 1] for j in range(0, len(parts), 2)]
            return acc + parts[0]
        acc = lax.fori_loop(0, nch, body, jnp.zeros((32, TILE), bf16))
        return jnp.sum(acc.astype(f32), axis=0, keepdims=True)

    def bisect_half(wanted, cnt_lo0):
        def body(_, carry):
            lo, hi, cnt_lo, cnt_hi = carry
            mid = (lo + hi) >> 1
            cnt = count_ge(mid)
            ge = cnt >= wanted
            return (jnp.where(ge, mid, lo), jnp.where(ge, hi, mid),
                    jnp.where(ge, cnt, cnt_lo), jnp.where(ge, cnt_hi, cnt))
        lo0 = jnp.full((1, TILE), HALF_MIN, i32)
        hi0 = jnp.full((1, TILE), HALF_MAX + 1, i32)
        lo, _, cnt_lo, cnt_hi = lax.fori_loop(
            0, HALF_BITS, body, (lo0, hi0, cnt_lo0, jnp.zeros((1, TILE), f32)))
        return lo, cnt_lo, cnt_hi

    k_f = jnp.full((1, TILE), float(top_k), f32)
    processed = (nch * CHUNK).astype(f32)
    thr_hi, cnt_ge_hi, cnt_gt_hi = bisect_half(k_f, jnp.full((1, TILE), processed, f32))

    def low_half_body(c, carry):
        key = key_ref[c]
        low = (key & jnp.int32(HALF_MAX - HALF_MIN)) + HALF_MIN
        in_bucket = (key >> HALF_BITS) == thr_hi
        half_ref[c] = jnp.where(in_bucket, low, HALF_MIN).astype(i16)
        return carry

    lax.fori_loop(0, nch, low_half_body, 0)
    thr_lo, cnt_ge_lo, cnt_gt_lo = bisect_half(k_f - cnt_gt_hi, cnt_ge_hi - cnt_gt_hi)
    thr = (thr_hi << HALF_BITS) + (thr_lo - HALF_MIN)
    cnt_gt = cnt_gt_hi + cnt_gt_lo
    cnt_ge = cnt_gt_hi + cnt_ge_lo
    ties_wanted = k_f - cnt_gt
    any_ties = jnp.max(cnt_ge - k_f) > 0.0

    def write_bias(c, sel):
        causal = (c * CHUNK + key_row) <= q_pos
        bias_ref[c] = jnp.where(sel & causal, 0.0, -jnp.inf)

    @pl.when(jnp.logical_not(any_ties))
    def _():
        def body(c, carry):
            write_bias(c, key_ref[c] >= thr)
            return carry
        lax.fori_loop(0, nch, body, 0)

    @pl.when(any_ties)
    def _():
        lower_tri = (lax.broadcasted_iota(i32, (CHUNK, CHUNK), 1)
                     <= lax.broadcasted_iota(i32, (CHUNK, CHUNK), 0))
        lower_tri = jnp.where(lower_tri, 1.0, 0.0).astype(bf16)

        def body(c, seen):
            key = key_ref[c]
            tie = key == thr
            tie_f = jnp.where(tie, 1.0, 0.0)
            rank = jnp.dot(lower_tri, tie_f.astype(bf16), preferred_element_type=f32) + seen
            write_bias(c, (key > thr) | (tie & (rank <= ties_wanted)))
            return seen + jnp.sum(tie_f, axis=0, keepdims=True)
        lax.fori_loop(0, nch, body, jnp.zeros((1, TILE), f32))

    m_ref[...] = jnp.full((N_HEADS, TILE), NEG_BIG, f32)
    acc_ref[...] = jnp.zeros((N_HEADS, V_ROWS, TILE), f32)

    def attn_body(c, carry):
        kc = k_ref[c]
        vc = vt_ref[c]
        bias = bias_ref[c]
        col_max = []
        for hd in range(N_HEADS):
            lt = jnp.dot(kc, qt_ref[hd * HEAD_DIM:(hd + 1) * HEAD_DIM, :],
                         preferred_element_type=f32) + bias
            lt_ref[hd] = lt
            col_max.append(jnp.max(lt, axis=0, keepdims=True))
        for hd in range(N_HEADS):
            m_old = m_ref[hd:hd + 1, :]
            m_new = jnp.maximum(m_old, col_max[hd])
            p = jnp.exp2(lt_ref[hd] - m_new).astype(bf16)
            alpha = jnp.exp2(m_old - m_new)
            acc_ref[hd] = acc_ref[hd] * alpha + jnp.dot(vc, p, preferred_element_type=f32)
            m_ref[hd:hd + 1, :] = m_new
        return carry

    lax.fori_loop(0, nch, attn_body, 0)
    for hd in range(N_HEADS):
        a = acc_ref[hd]
        attn_ref[hd * HEAD_DIM:(hd + 1) * HEAD_DIM, :] = a[0:HEAD_DIM] / a[HEAD_DIM:HEAD_DIM + 1]
    attn = attn_ref[...].T

    u_a = jnp.dot(hb, wn_ref[:, N_UA:N_ZA], preferred_element_type=f32)

    @pl.when(i == 0)
    def _():
        ext_ref[0:MAX_WINDOW, :] = jnp.zeros((MAX_WINDOW, POOL_WIDTH), f32)

    @pl.when(i > 0)
    def _():
        ext_ref[0:MAX_WINDOW, :] = ext_ref[TILE:TILE + MAX_WINDOW, :]

    ext_ref[MAX_WINDOW:MAX_WINDOW + TILE, :] = u_a
    t_pos = q0 + lax.broadcasted_iota(i32, (TILE, 1), 0)
    mixed = []
    for g, w in enumerate(POOL_WINDOWS):
        cols = slice(g * POOL_GROUP_DIM, (g + 1) * POOL_GROUP_DIM)
        win = ext_ref[MAX_WINDOW:MAX_WINDOW + TILE, cols]
        for j in range(1, w):
            win = win + ext_ref[MAX_WINDOW - j:MAX_WINDOW - j + TILE, cols]
        count = jnp.minimum(t_pos + 1, w).astype(f32)
        pooled = win / count - u_a[:, cols]
        mixed.append(jnp.dot(pooled.astype(bf16), wpool_ref[g], preferred_element_type=f32))
    mixed = jnp.concatenate(mixed, axis=1) * pscale_ref[...]
    z_a = jnp.dot(hb, wn_ref[:, N_ZA:N_ZB], preferred_element_type=f32)
    y_a = mixed * (z_a * jax.nn.sigmoid(z_a))
    a_out = jnp.dot(y_a.astype(bf16), wa_ref[...], preferred_element_type=f32)

    z_b = jnp.dot(hb, wn_ref[:, N_ZB:N_GA], preferred_element_type=f32)
    y_b = attn * (z_b * jax.nn.sigmoid(z_b))
    b_out = jnp.dot(y_b.astype(bf16), wb_ref[...], preferred_element_type=f32)
    g_a = jnp.dot(hb, wn_ref[:, N_GA:N_GB], preferred_element_type=f32)
    g_b = jnp.dot(hb, wn_ref[:, N_GB:N_END], preferred_element_type=f32)
    merged = jax.nn.sigmoid(g_a) * a_out + jax.nn.sigmoid(g_b) * b_out
    y = jnp.dot(merged.astype(bf16), wo_ref[...], preferred_element_type=f32)
    out_ref[0] = x + gate * y


def _const_spec(shape):
    zeros = (0,) * len(shape)
    return pl.BlockSpec(shape, lambda b, i: zeros)


def _layer(x, mod3, pos3, gain, invf, w_n, w_t, w_pool, pscale, qg, kg, w_a, w_b, w_o):
    batch, seq_len, d = x.shape
    assert d == D_MODEL and seq_len % TILE == 0 and TILE % CHUNK == 0
    assert seq_len // 32 <= 256, "bf16 partial counts must stay exactly representable"
    n_chunks = seq_len // CHUNK
    top_k = min(TOPK_MAX, seq_len // 4)
    bf16, f32, i32 = jnp.bfloat16, jnp.float32, jnp.int32
    kernel = functools.partial(_layer_kernel, seq_len=seq_len, top_k=top_k)
    return pl.pallas_call(
        kernel,
        grid=(batch, seq_len // TILE),
        in_specs=[
            pl.BlockSpec((1, TILE, D_MODEL), lambda b, i: (b, i, 0)),
            pl.BlockSpec((1, 3, D_MODEL), lambda b, i: (b, 0, 0)),
            pl.BlockSpec((1, 1, TILE), lambda b, i: (b, 0, i)),
            _const_spec(gain.shape), _const_spec(invf.shape), _const_spec(w_n.shape),
            _const_spec(w_t.shape), _const_spec(w_pool.shape), _const_spec(pscale.shape),
            _const_spec(qg.shape), _const_spec(kg.shape), _const_spec(w_a.shape),
            _const_spec(w_b.shape), _const_spec(w_o.shape),
        ],
        out_specs=pl.BlockSpec((1, TILE, D_MODEL), lambda b, i: (b, i, 0)),
        out_shape=jax.ShapeDtypeStruct(x.shape, x.dtype),
        scratch_shapes=[
            pltpu.VMEM((TILE + MAX_WINDOW, POOL_WIDTH), f32),
            pltpu.VMEM((n_chunks, CHUNK, HEAD_DIM), bf16),
            pltpu.VMEM((n_chunks, CHUNK, IDX_DIM), bf16),
            pltpu.VMEM((n_chunks, V_ROWS, CHUNK), bf16),
            pltpu.VMEM((ATTN_WIDTH, TILE), bf16),
            pltpu.VMEM((IDX_HEADS * IDX_DIM, TILE), bf16),
            pltpu.VMEM((IDX_HEADS, TILE), f32),
            pltpu.VMEM((n_chunks, CHUNK, TILE), i32),
            pltpu.VMEM((n_chunks, CHUNK, TILE), f32),
            pltpu.VMEM((N_HEADS, TILE), f32),
            pltpu.VMEM((N_HEADS, V_ROWS, TILE), f32),
            pltpu.VMEM((ATTN_WIDTH, TILE), f32),
            pltpu.VMEM((N_HEADS, CHUNK, TILE), f32),
            pltpu.VMEM((n_chunks, CHUNK, TILE), jnp.int16),
        ],
        compiler_params=pltpu.CompilerParams(
            dimension_semantics=("arbitrary", "arbitrary"),
            vmem_limit_bytes=VMEM_LIMIT_BYTES),
        name="hybrid_layer",
    )(x, mod3, pos3, gain, invf, w_n, w_t, w_pool, pscale, qg, kg, w_a, w_b, w_o)


def _modulation(c, w_ada, b_ada):
    batch, d = c.shape
    n_out = w_ada.shape[1]
    blk = D_MODEL
    return pl.pallas_call(
        _mod_kernel,
        grid=(n_out // blk,),
        in_specs=[pl.BlockSpec((batch, d), lambda j: (0, 0)),
                  pl.BlockSpec((d, blk), lambda j: (0, j)),
                  pl.BlockSpec((1, blk), lambda j: (0, j))],
        out_specs=pl.BlockSpec((batch, blk), lambda j: (0, j)),
        out_shape=jax.ShapeDtypeStruct((batch, n_out), jnp.float32),
        name="adaln_mod",
    )(c, w_ada, b_ada.reshape(1, n_out))


def kernel(x, c, positions, w_ada, b_ada, norm_gain, w_in, w_pool_group, pool_scale,
           q_norm_gain, k_norm_gain, w_branch_a, w_branch_b, w_out):
    batch, seq_len, _ = x.shape
    bf16 = jnp.bfloat16
    inv_freq = ROPE_THETA ** (-jnp.arange(0, ROT_DIM, 2, dtype=jnp.float32) / ROT_DIM)
    invf = inv_freq.reshape(ROT_HALF, 1)
    pos3 = positions.reshape(batch, 1, seq_len)
    bounds = [0]
    for s in IN_SIZES:
        bounds.append(bounds[-1] + s)
    depth = w_in.shape[0]
    for layer in range(depth):
        seg = [w_in[layer][:, bounds[j]:bounds[j + 1]] for j in range(len(IN_SIZES))]
        u_a, z_a, q, k, v, z_b, q_idx, k_idx, w_idx, g_a, g_b = seg
        w_n = jnp.concatenate([u_a, z_a, z_b, g_a, g_b], axis=1).astype(bf16)
        pad = jnp.zeros((D_MODEL, T_END - T_W - IDX_HEADS), jnp.float32)
        w_t = jnp.concatenate([q, q_idx, k, k_idx, v, w_idx, pad], axis=1).T.astype(bf16)
        mod = _modulation(c, w_ada[layer], b_ada[layer])
        x = _layer(
            x, mod.reshape(batch, 3, D_MODEL), pos3, norm_gain[layer].reshape(1, D_MODEL), invf,
            w_n, w_t, w_pool_group[layer].astype(bf16), pool_scale[layer].reshape(1, POOL_WIDTH),
            q_norm_gain[layer].reshape(HEAD_DIM, 1), k_norm_gain[layer].reshape(HEAD_DIM, 1),
            w_branch_a[layer].astype(bf16), w_branch_b[layer].astype(bf16), w_out[layer].astype(bf16))
    return x
```

```python
import functools

import jax
import jax.numpy as jnp
from jax import lax
from jax.experimental import pallas as pl
from jax.experimental.pallas import tpu as pltpu

D_MODEL = 1024
POOL_WIDTH = 512
POOL_GROUPS = 4
POOL_GROUP_DIM = 128
POOL_WINDOWS = (2, 4, 8, 16)
MAX_WINDOW = 16
N_HEADS = 8
HEAD_DIM = 64
ATTN_WIDTH = N_HEADS * HEAD_DIM
IDX_HEADS = 8
IDX_DIM = 64
TOPK_MAX = 256
ROPE_THETA = 500000.0
ROT_DIM = HEAD_DIM // 4
ROT_HALF = ROT_DIM // 2
EPS = 1e-6
IN_SIZES = (POOL_WIDTH, POOL_WIDTH, ATTN_WIDTH, HEAD_DIM, HEAD_DIM, ATTN_WIDTH,
            IDX_HEADS * IDX_DIM, IDX_DIM, IDX_HEADS, D_MODEL, D_MODEL)

TILE = 256
CHUNK = 256
V_ROWS = 80
NEG_BIG = -1e30
LOG2_E = 1.4426950408889634
HALF_BITS = 16
HALF_MIN, HALF_MAX = -(1 << 15), (1 << 15) - 1
VMEM_LIMIT_BYTES = 56 * 1024 * 1024

T_Q, T_K, T_V, T_END = 0, 512, 576, 640
I_Q, I_K, I_W, I_END = 0, 512, 576, 592
IDX_K = 6 * IDX_DIM
N_UA, N_ZA, N_ZB, N_GA, N_GB, N_END = 0, 512, 1024, 1536, 2560, 3584
SLAB = 256
N_SLABS = N_END // SLAB
assert N_SLABS <= HALF_BITS

_NT_DIMS = (((1,), (1,)), ((), ()))


def _mod_kernel(c_ref, w_ref, b_ref, o_ref):
    c = c_ref[...]
    s = c * jax.nn.sigmoid(c)
    o_ref[...] = jnp.dot(s, w_ref[...], preferred_element_type=jnp.float32,
                         precision=lax.Precision.HIGHEST) + b_ref[...]


def _rope_rows(blk, cos, sin):
    x1 = blk[0:ROT_HALF]
    x2 = blk[ROT_HALF:ROT_DIM]
    return jnp.concatenate([x1 * cos - x2 * sin, x2 * cos + x1 * sin, blk[ROT_DIM:]], axis=0)


def _split3(x):
    hi = x.astype(jnp.bfloat16).astype(jnp.float32)
    rest = x - hi
    mid = rest.astype(jnp.bfloat16).astype(jnp.float32)
    low = (rest - mid).astype(jnp.bfloat16).astype(jnp.float32)
    return hi, mid, low


def _rms_rows(blk, gain_col):
    ms = jnp.mean(blk * blk, axis=0, keepdims=True)
    return blk * lax.rsqrt(ms + EPS) * gain_col


def _layer_kernel(x_ref, mod_ref, pos_ref, gain_ref, invf_ref, wn_ref, wt_ref, wti_ref, wpool_ref,
                  pscale_ref, qg_ref, kg_ref, wa_ref, wb_ref, wo_ref, out_ref,
                  ext_ref, k_ref, ki_ref, vt_ref, qt_ref, qit_ref, widx_ref, key_ref, bias_ref,
                  m_ref, acc_ref, attn_ref, lta_ref, ltb_ref, half_ref, hb_ref, pn_ref, thr_ref,
                  cnt_ref, *, seq_len, top_k):
    i = pl.program_id(1)
    q0 = i * TILE
    nch = (q0 + TILE) // CHUNK
    f32, bf16, i32, i16 = jnp.float32, jnp.bfloat16, jnp.int32, jnp.int16

    x = x_ref[0]
    shift = mod_ref[0, 0:1, :]
    scale = mod_ref[0, 1:2, :]
    gate = mod_ref[0, 2:3, :]
    ms = jnp.mean(x * x, axis=-1, keepdims=True)
    h = (x * lax.rsqrt(ms + EPS) * gain_ref[...]) * (1.0 + scale) + shift
    hb = h.astype(bf16)
    hb_ref[...] = hb

    pt = lax.dot_general(wt_ref[...], hb, _NT_DIMS, preferred_element_type=f32)
    h_mid_f = h - hb.astype(f32)
    h_mid = h_mid_f.astype(bf16)
    h_low = (h_mid_f - h_mid.astype(f32)).astype(bf16)
    p_hi = lax.dot_general(wti_ref[...], hb, _NT_DIMS, preferred_element_type=f32)
    p_mid = lax.dot_general(wti_ref[0:2 * I_END, :], h_mid, _NT_DIMS, preferred_element_type=f32)
    p_low = lax.dot_general(wti_ref[0:I_END, :], h_low, _NT_DIMS, preferred_element_type=f32)
    pti = (((p_mid[I_END:2 * I_END] + p_hi[2 * I_END:3 * I_END]) + p_low)
           + (p_mid[0:I_END] + p_hi[I_END:2 * I_END])) + p_hi[0:I_END]
    ang = invf_ref[...] * pos_ref[0].astype(f32)
    cos = jnp.cos(ang)
    sin = jnp.sin(ang)
    attn_scale = HEAD_DIM ** -0.5 * LOG2_E
    for hd in range(N_HEADS):
        blk = pt[T_Q + hd * HEAD_DIM:T_Q + (hd + 1) * HEAD_DIM]
        blk = _rope_rows(_rms_rows(blk, qg_ref[...]), cos, sin) * attn_scale
        qt_ref[hd * HEAD_DIM:(hd + 1) * HEAD_DIM, :] = blk.astype(bf16)
        q_hi, q_mid, q_low = _split3(_rope_rows(pti[I_Q + hd * IDX_DIM:I_Q + (hd + 1) * IDX_DIM], cos, sin))
        qit_ref[hd * IDX_K:(hd + 1) * IDX_K, :] = jnp.concatenate(
            [q_hi, q_mid, q_hi, q_low, q_hi, q_mid], axis=0).astype(bf16)
    kt = _rope_rows(_rms_rows(pt[T_K:T_V], kg_ref[...]), cos, sin)
    k_hi, k_mid, k_low = _split3(_rope_rows(pti[I_K:I_W], cos, sin))
    kk = jnp.concatenate([k_hi, k_hi, k_mid, k_hi, k_low, k_mid, kt, jnp.zeros_like(kt)], axis=0)
    kk = kk.T.astype(bf16)
    row = lax.broadcasted_iota(i32, (V_ROWS - HEAD_DIM, CHUNK), 0)
    ones_rows = jnp.where(row == 0, 1.0, 0.0).astype(bf16)
    for j in range(TILE // CHUNK):
        c = i * (TILE // CHUNK) + j
        ki_ref[c] = kk[j * CHUNK:(j + 1) * CHUNK, 0:IDX_K]
        k_ref[c] = kk[j * CHUNK:(j + 1) * CHUNK, IDX_K:IDX_K + HEAD_DIM]
        vt_ref[c, 0:HEAD_DIM, :] = pt[T_V:T_END, j * CHUNK:(j + 1) * CHUNK].astype(bf16)
        vt_ref[c, HEAD_DIM:V_ROWS, :] = ones_rows
    idx_w_scale = (IDX_HEADS ** -0.5) * (IDX_DIM ** -0.5)
    widx_ref[...] = pti[I_W:I_W + IDX_HEADS] * idx_w_scale

    key_row = lax.broadcasted_iota(i32, (CHUNK, TILE), 0)
    q_pos = q0 + lax.broadcasted_iota(i32, (CHUNK, TILE), 1)

    def score_body(c, carry):
        kic = ki_ref[c]
        score = jnp.zeros((CHUNK, TILE), f32)
        for hd in range(IDX_HEADS):
            s = jnp.dot(kic, qit_ref[hd * IDX_K:(hd + 1) * IDX_K, :], preferred_element_type=f32)
            score = score + jnp.maximum(s, 0.0) * widx_ref[hd:hd + 1, :]
        causal = (c * CHUNK + key_row) <= q_pos
        score = jnp.where(causal, score, -jnp.inf)
        score = jnp.where(score == 0.0, 0.0, score)
        bits = pltpu.bitcast(score, i32)
        key = bits ^ ((bits >> 31) & jnp.int32(0x7FFFFFFF))
        key_ref[c] = key
        half_ref[c] = (key >> HALF_BITS).astype(i16)
        return carry

    lax.fori_loop(0, nch, score_body, 0)

    k_f = jnp.full((1, TILE), float(top_k), f32)

    def search(n_chunks):
        def count_ge(mid):
            mid16 = mid.astype(i16)
            acc = None
            for c in range(n_chunks):
                one = jnp.where(half_ref[c] >= mid16, jnp.bfloat16(1), jnp.bfloat16(0))
                parts = [one[s * 32:(s + 1) * 32] for s in range(CHUNK // 32)]
                while len(parts) > 1:
                    parts = [parts[j] + parts[j + 1] for j in range(0, len(parts), 2)]
                acc = parts[0] if acc is None else acc + parts[0]
            return jnp.sum(acc.astype(f32), axis=0, keepdims=True)

        def bisect_half(wanted, cnt_lo0, proj_steps):
            def step(carry):
                lo, hi, cnt_lo, cnt_hi = carry
                mid = (lo + hi) >> 1
                cnt = count_ge(mid)
                ge = cnt >= wanted
                return (jnp.where(ge, mid, lo), jnp.where(ge, hi, mid),
                        jnp.where(ge, cnt, cnt_lo), jnp.where(ge, cnt_hi, cnt))

            def step_with_proj(t, carry):
                for r in range(2):
                    rows = slice(r * TILE // 2, (r + 1) * TILE // 2)
                    pn_ref[t, rows, :] = jnp.dot(hb_ref[rows, :], wn_ref[t], preferred_element_type=f32)
                return step(carry)

            lo0 = jnp.full((1, TILE), HALF_MIN, i32)
            hi0 = jnp.full((1, TILE), HALF_MAX + 1, i32)
            carry = (lo0, hi0, cnt_lo0, jnp.zeros((1, TILE), f32))
            if proj_steps:
                for t in range(HALF_BITS):
                    carry = step_with_proj(t, carry) if t < proj_steps else step(carry)
            else:
                carry = lax.fori_loop(0, HALF_BITS, lambda _, cr: step(cr), carry)
            lo, _, cnt_lo, cnt_hi = carry
            return lo, cnt_lo, cnt_hi

        processed = jnp.full((1, TILE), float(n_chunks * CHUNK), f32)
        thr_hi, cnt_ge_hi, cnt_gt_hi = bisect_half(k_f, processed, N_SLABS)
        for c in range(n_chunks):
            key = key_ref[c]
            low = (key & jnp.int32(HALF_MAX - HALF_MIN)) + HALF_MIN
            in_bucket = (key >> HALF_BITS) == thr_hi
            half_ref[c] = jnp.where(in_bucket, low, HALF_MIN).astype(i16)
        thr_lo, cnt_ge_lo, cnt_gt_lo = bisect_half(k_f - cnt_gt_hi, cnt_ge_hi - cnt_gt_hi, 0)
        thr_ref[...] = (thr_hi << HALF_BITS) + (thr_lo - HALF_MIN)
        cnt_ref[0:1, :] = cnt_gt_hi + cnt_gt_lo
        cnt_ref[1:2, :] = cnt_gt_hi + cnt_ge_lo

    for tile_idx in range(seq_len // TILE):
        pl.when(i == tile_idx)(functools.partial(search, (tile_idx + 1) * TILE // CHUNK))
    thr = thr_ref[...]
    cnt_gt = cnt_ref[0:1, :]
    cnt_ge = cnt_ref[1:2, :]
    ties_wanted = k_f - cnt_gt
    any_ties = jnp.max(cnt_ge - k_f) > 0.0

    def write_bias(c, sel):
        causal = (c * CHUNK + key_row) <= q_pos
        bias_ref[c] = jnp.where(sel & causal, 0.0, -jnp.inf)

    @pl.when(jnp.logical_not(any_ties))
    def _():
        def body(c, carry):
            write_bias(c, key_ref[c] >= thr)
            return carry
        lax.fori_loop(0, nch, body, 0)

    @pl.when(any_ties)
    def _():
        lower_tri = (lax.broadcasted_iota(i32, (CHUNK, CHUNK), 1)
                     <= lax.broadcasted_iota(i32, (CHUNK, CHUNK), 0))
        lower_tri = jnp.where(lower_tri, 1.0, 0.0).astype(bf16)

        def body(c, seen):
            key = key_ref[c]
            tie = key == thr
            tie_f = jnp.where(tie, 1.0, 0.0)
            rank = jnp.dot(lower_tri, tie_f.astype(bf16), preferred_element_type=f32) + seen
            write_bias(c, (key > thr) | (tie & (rank <= ties_wanted)))
            return seen + jnp.sum(tie_f, axis=0, keepdims=True)
        lax.fori_loop(0, nch, body, jnp.zeros((1, TILE), f32))

    m_ref[...] = jnp.full((N_HEADS, TILE), NEG_BIG, f32)
    acc_ref[...] = jnp.zeros((N_HEADS, V_ROWS, TILE), f32)

    def attn_logits(c, buf_ref):
        kc = k_ref[c]
        bias = bias_ref[c]
        col_max = []
        for hd in range(N_HEADS):
            lt = jnp.dot(kc, qt_ref[hd * HEAD_DIM:(hd + 1) * HEAD_DIM, :],
                         preferred_element_type=f32) + bias
            buf_ref[hd] = lt
            col_max.append(jnp.max(lt, axis=0, keepdims=True))
        return tuple(col_max)

    def attn_accumulate(c, buf_ref, col_max):
        vc = vt_ref[c]
        for hd in range(N_HEADS):
            m_old = m_ref[hd:hd + 1, :]
            m_new = jnp.maximum(m_old, col_max[hd])
            p = jnp.exp2(buf_ref[hd] - m_new).astype(bf16)
            alpha = jnp.exp2(m_old - m_new)
            acc_ref[hd] = acc_ref[hd] * alpha + jnp.dot(vc, p, preferred_element_type=f32)
            m_ref[hd:hd + 1, :] = m_new

    def attn_pair(p, max_a):
        c = 2 * p
        max_b = attn_logits(c + 1, ltb_ref)
        attn_accumulate(c, lta_ref, max_a)
        max_a = attn_logits(c + 2, lta_ref)
        attn_accumulate(c + 1, ltb_ref, max_b)
        return max_a

    n_pairs = (nch - 1) // 2
    max_a = lax.fori_loop(0, n_pairs, attn_pair, attn_logits(0, lta_ref))
    c_last = 2 * n_pairs

    @pl.when(c_last == nch - 1)
    def _():
        attn_accumulate(c_last, lta_ref, max_a)

    @pl.when(c_last != nch - 1)
    def _():
        max_b = attn_logits(c_last + 1, ltb_ref)
        attn_accumulate(c_last, lta_ref, max_a)
        attn_accumulate(c_last + 1, ltb_ref, max_b)
    for hd in range(N_HEADS):
        a = acc_ref[hd]
        attn_ref[hd * HEAD_DIM:(hd + 1) * HEAD_DIM, :] = a[0:HEAD_DIM] / a[HEAD_DIM:HEAD_DIM + 1]
    attn = attn_ref[...].T

    def proj_cols(start, stop):
        return jnp.concatenate([pn_ref[s] for s in range(start // SLAB, stop // SLAB)], axis=1)

    u_a = proj_cols(N_UA, N_ZA)

    @pl.when(i == 0)
    def _():
        ext_ref[0:MAX_WINDOW, :] = jnp.zeros((MAX_WINDOW, POOL_WIDTH), f32)

    @pl.when(i > 0)
    def _():
        ext_ref[0:MAX_WINDOW, :] = ext_ref[TILE:TILE + MAX_WINDOW, :]

    ext_ref[MAX_WINDOW:MAX_WINDOW + TILE, :] = u_a
    t_pos = q0 + lax.broadcasted_iota(i32, (TILE, 1), 0)
    mixed = []
    for g, w in enumerate(POOL_WINDOWS):
        cols = slice(g * POOL_GROUP_DIM, (g + 1) * POOL_GROUP_DIM)
        win = ext_ref[MAX_WINDOW:MAX_WINDOW + TILE, cols]
        for j in range(1, w):
            win = win + ext_ref[MAX_WINDOW - j:MAX_WINDOW - j + TILE, cols]
        count = jnp.minimum(t_pos + 1, w).astype(f32)
        pooled = win / count - u_a[:, cols]
        mixed.append(jnp.dot(pooled.astype(bf16), wpool_ref[g], preferred_element_type=f32))
    mixed = jnp.concatenate(mixed, axis=1) * pscale_ref[...]
    z_a = proj_cols(N_ZA, N_ZB)
    y_a = mixed * (z_a * jax.nn.sigmoid(z_a))
    a_out = jnp.dot(y_a.astype(bf16), wa_ref[...], preferred_element_type=f32)

    z_b = proj_cols(N_ZB, N_GA)
    y_b = attn * (z_b * jax.nn.sigmoid(z_b))
    b_out = jnp.dot(y_b.astype(bf16), wb_ref[...], preferred_element_type=f32)
    g_a = proj_cols(N_GA, N_GB)
    g_b = proj_cols(N_GB, N_END)
    merged = jax.nn.sigmoid(g_a) * a_out + jax.nn.sigmoid(g_b) * b_out
    y = jnp.dot(merged.astype(bf16), wo_ref[...], preferred_element_type=f32)
    out_ref[0] = x + gate * y


def _const_spec(shape):
    zeros = (0,) * len(shape)
    return pl.BlockSpec(shape, lambda b, i: zeros, pipeline_mode=pl.Buffered(1))


def _layer(x, mod3, pos3, gain, invf, w_n, w_t, w_ti, w_pool, pscale, qg, kg, w_a, w_b, w_o):
    batch, seq_len, d = x.shape
    assert d == D_MODEL and seq_len % TILE == 0 and TILE % CHUNK == 0
    assert seq_len // 32 <= 256, "bf16 partial counts must stay exactly representable"
    n_chunks = seq_len // CHUNK
    top_k = min(TOPK_MAX, seq_len // 4)
    bf16, f32, i32 = jnp.bfloat16, jnp.float32, jnp.int32
    kernel = functools.partial(_layer_kernel, seq_len=seq_len, top_k=top_k)
    return pl.pallas_call(
        kernel,
        grid=(batch, seq_len // TILE),
        in_specs=[
            pl.BlockSpec((1, TILE, D_MODEL), lambda b, i: (b, i, 0)),
            pl.BlockSpec((1, 3, D_MODEL), lambda b, i: (b, 0, 0)),
            pl.BlockSpec((1, 1, TILE), lambda b, i: (b, 0, i)),
            _const_spec(gain.shape), _const_spec(invf.shape), _const_spec(w_n.shape),
            _const_spec(w_t.shape), _const_spec(w_ti.shape), _const_spec(w_pool.shape), _const_spec(pscale.shape),
            _const_spec(qg.shape), _const_spec(kg.shape), _const_spec(w_a.shape),
            _const_spec(w_b.shape), _const_spec(w_o.shape),
        ],
        out_specs=pl.BlockSpec((1, TILE, D_MODEL), lambda b, i: (b, i, 0)),
        out_shape=jax.ShapeDtypeStruct(x.shape, x.dtype),
        scratch_shapes=[
            pltpu.VMEM((TILE + MAX_WINDOW, POOL_WIDTH), f32),
            pltpu.VMEM((n_chunks, CHUNK, HEAD_DIM), bf16),
            pltpu.VMEM((n_chunks, CHUNK, IDX_K), bf16),
            pltpu.VMEM((n_chunks, V_ROWS, CHUNK), bf16),
            pltpu.VMEM((ATTN_WIDTH, TILE), bf16),
            pltpu.VMEM((IDX_HEADS * IDX_K, TILE), bf16),
            pltpu.VMEM((IDX_HEADS, TILE), f32),
            pltpu.VMEM((n_chunks, CHUNK, TILE), i32),
            pltpu.VMEM((n_chunks, CHUNK, TILE), f32),
            pltpu.VMEM((N_HEADS, TILE), f32),
            pltpu.VMEM((N_HEADS, V_ROWS, TILE), f32),
            pltpu.VMEM((ATTN_WIDTH, TILE), f32),
            pltpu.VMEM((N_HEADS, CHUNK, TILE), f32),
            pltpu.VMEM((N_HEADS, CHUNK, TILE), f32),
            pltpu.VMEM((n_chunks, CHUNK, TILE), jnp.int16),
            pltpu.VMEM((TILE, D_MODEL), bf16),
            pltpu.VMEM((N_SLABS, TILE, SLAB), f32),
            pltpu.VMEM((1, TILE), i32),
            pltpu.VMEM((2, TILE), f32),
        ],
        compiler_params=pltpu.CompilerParams(
            dimension_semantics=("arbitrary", "arbitrary"),
            vmem_limit_bytes=VMEM_LIMIT_BYTES),
        name="hybrid_layer",
    )(x, mod3, pos3, gain, invf, w_n, w_t, w_ti, w_pool, pscale, qg, kg, w_a, w_b, w_o)


def _modulation(c, w_ada, b_ada):
    batch, d = c.shape
    n_out = w_ada.shape[1]
    blk = D_MODEL
    return pl.pallas_call(
        _mod_kernel,
        grid=(n_out // blk,),
        in_specs=[pl.BlockSpec((batch, d), lambda j: (0, 0)),
                  pl.BlockSpec((d, blk), lambda j: (0, j)),
                  pl.BlockSpec((1, blk), lambda j: (0, j))],
        out_specs=pl.BlockSpec((batch, blk), lambda j: (0, j)),
        out_shape=jax.ShapeDtypeStruct((batch, n_out), jnp.float32),
        name="adaln_mod",
    )(c, w_ada, b_ada.reshape(1, n_out))


def kernel(x, c, positions, w_ada, b_ada, norm_gain, w_in, w_pool_group, pool_scale,
           q_norm_gain, k_norm_gain, w_branch_a, w_branch_b, w_out):
    batch, seq_len, _ = x.shape
    bf16 = jnp.bfloat16
    inv_freq = ROPE_THETA ** (-jnp.arange(0, ROT_DIM, 2, dtype=jnp.float32) / ROT_DIM)
    invf = inv_freq.reshape(ROT_HALF, 1)
    pos3 = positions.reshape(batch, 1, seq_len)
    bounds = [0]
    for s in IN_SIZES:
        bounds.append(bounds[-1] + s)
    depth = w_in.shape[0]
    for layer in range(depth):
        seg = [w_in[layer][:, bounds[j]:bounds[j + 1]] for j in range(len(IN_SIZES))]
        u_a, z_a, q, k, v, z_b, q_idx, k_idx, w_idx, g_a, g_b = seg
        w_n = jnp.concatenate([u_a, z_a, z_b, g_a, g_b], axis=1).astype(bf16)
        w_n = w_n.reshape(D_MODEL, N_SLABS, SLAB).transpose(1, 0, 2)
        w_t = jnp.concatenate([q, k, v], axis=1).T.astype(bf16)
        pad = jnp.zeros((D_MODEL, I_END - I_W - IDX_HEADS), jnp.float32)
        w_ti = jnp.concatenate(_split3(jnp.concatenate([q_idx, k_idx, w_idx, pad], axis=1).T),
                               axis=0).astype(bf16)
        mod = _modulation(c, w_ada[layer], b_ada[layer])
        x = _layer(
            x, mod.reshape(batch, 3, D_MODEL), pos3, norm_gain[layer].reshape(1, D_MODEL), invf,
            w_n, w_t, w_ti, w_pool_group[layer].astype(bf16), pool_scale[layer].reshape(1, POOL_WIDTH),
            q_norm_gain[layer].reshape(HEAD_DIM, 1), k_norm_gain[layer].reshape(HEAD_DIM, 1),
            w_branch_a[layer].astype(bf16), w_branch_b[layer].astype(bf16), w_out[layer].astype(bf16))
    return x
```

```python
import functools

import jax
import jax.numpy as jnp
from jax import lax
from jax.experimental import pallas as pl
from jax.experimental.pallas import tpu as pltpu

D_MODEL = 1024
POOL_WIDTH = 512
POOL_GROUPS = 4
POOL_GROUP_DIM = 128
POOL_WINDOWS = (2, 4, 8, 16)
MAX_WINDOW = 16
N_HEADS = 8
HEAD_DIM = 64
ATTN_WIDTH = N_HEADS * HEAD_DIM
IDX_HEADS = 8
IDX_DIM = 64
TOPK_MAX = 256
ROPE_THETA = 500000.0
ROT_DIM = HEAD_DIM // 4
ROT_HALF = ROT_DIM // 2
EPS = 1e-6
IN_SIZES = (POOL_WIDTH, POOL_WIDTH, ATTN_WIDTH, HEAD_DIM, HEAD_DIM, ATTN_WIDTH,
            IDX_HEADS * IDX_DIM, IDX_DIM, IDX_HEADS, D_MODEL, D_MODEL)

TILE = 256
CHUNK = 256
V_ROWS = 80
NEG_BIG = -1e30
LOG2_E = 1.4426950408889634
HALF_BITS = 16
HALF_MIN, HALF_MAX = -(1 << 15), (1 << 15) - 1
VMEM_LIMIT_BYTES = 56 * 1024 * 1024

T_Q, T_K, T_V, T_END = 0, 512, 576, 640
I_Q, I_K, I_W, I_END = 0, 512, 576, 592
IDX_K = 6 * IDX_DIM
N_UA, N_ZA, N_ZB, N_GA, N_GB, N_END = 0, 512, 1024, 1536, 2560, 3584
SLAB = 256
N_SLABS = N_END // SLAB
assert N_SLABS <= HALF_BITS

_NT_DIMS = (((1,), (1,)), ((), ()))


def _mod_kernel(c_ref, w_ref, b_ref, o_ref):
    c = c_ref[...]
    s = c * jax.nn.sigmoid(c)
    o_ref[...] = jnp.dot(s, w_ref[...], preferred_element_type=jnp.float32,
                         precision=lax.Precision.HIGHEST) + b_ref[...]


def _rope_rows(blk, cos, sin):
    x1 = blk[0:ROT_HALF]
    x2 = blk[ROT_HALF:ROT_DIM]
    return jnp.concatenate([x1 * cos - x2 * sin, x2 * cos + x1 * sin, blk[ROT_DIM:]], axis=0)


def _split3(x):
    hi = x.astype(jnp.bfloat16).astype(jnp.float32)
    rest = x - hi
    mid = rest.astype(jnp.bfloat16).astype(jnp.float32)
    low = (rest - mid).astype(jnp.bfloat16).astype(jnp.float32)
    return hi, mid, low


def _rms_rows(blk, gain_col):
    ms = jnp.mean(blk * blk, axis=0, keepdims=True)
    return blk * lax.rsqrt(ms + EPS) * gain_col


def _pipelined_chunks(n_chunks, produce, consume, buf_a, buf_b):
    def pair(p, carried_a):
        c = 2 * p
        carried_b = produce(c + 1, buf_b)
        consume(c, buf_a, carried_a)
        carried_a = produce(c + 2, buf_a)
        consume(c + 1, buf_b, carried_b)
        return carried_a

    n_pairs = (n_chunks - 1) // 2
    carried_a = lax.fori_loop(0, n_pairs, pair, produce(0, buf_a))
    c_last = 2 * n_pairs

    @pl.when(c_last == n_chunks - 1)
    def _():
        consume(c_last, buf_a, carried_a)

    @pl.when(c_last != n_chunks - 1)
    def _():
        carried_b = produce(c_last + 1, buf_b)
        consume(c_last, buf_a, carried_a)
        consume(c_last + 1, buf_b, carried_b)


def _layer_kernel(x_ref, mod_ref, pos_ref, gain_ref, invf_ref, wn_ref, wt_ref, wti_ref, wpool_ref,
                  pscale_ref, qg_ref, kg_ref, wa_ref, wb_ref, wo_ref, out_ref,
                  ext_ref, k_ref, ki_ref, vt_ref, qt_ref, qit_ref, widx_ref, key_ref, bias_ref,
                  m_ref, acc_ref, attn_ref, lta_ref, ltb_ref, half_ref, hb_ref, pn_ref, thr_ref,
                  cnt_ref, *, seq_len, top_k):
    i = pl.program_id(1)
    q0 = i * TILE
    nch = (q0 + TILE) // CHUNK
    f32, bf16, i32, i16 = jnp.float32, jnp.bfloat16, jnp.int32, jnp.int16

    x = x_ref[0]
    shift = mod_ref[0, 0:1, :]
    scale = mod_ref[0, 1:2, :]
    gate = mod_ref[0, 2:3, :]
    ms = jnp.mean(x * x, axis=-1, keepdims=True)
    h = (x * lax.rsqrt(ms + EPS) * gain_ref[...]) * (1.0 + scale) + shift
    hb = h.astype(bf16)
    hb_ref[...] = hb

    pt = lax.dot_general(wt_ref[...], hb, _NT_DIMS, preferred_element_type=f32)
    h_mid_f = h - hb.astype(f32)
    h_mid = h_mid_f.astype(bf16)
    h_low = (h_mid_f - h_mid.astype(f32)).astype(bf16)
    p_hi = lax.dot_general(wti_ref[...], hb, _NT_DIMS, preferred_element_type=f32)
    p_mid = lax.dot_general(wti_ref[0:2 * I_END, :], h_mid, _NT_DIMS, preferred_element_type=f32)
    p_low = lax.dot_general(wti_ref[0:I_END, :], h_low, _NT_DIMS, preferred_element_type=f32)
    pti = (((p_mid[I_END:2 * I_END] + p_hi[2 * I_END:3 * I_END]) + p_low)
           + (p_mid[0:I_END] + p_hi[I_END:2 * I_END])) + p_hi[0:I_END]
    ang = invf_ref[...] * pos_ref[0].astype(f32)
    cos = jnp.cos(ang)
    sin = jnp.sin(ang)
    attn_scale = HEAD_DIM ** -0.5 * LOG2_E
    for hd in range(N_HEADS):
        blk = pt[T_Q + hd * HEAD_DIM:T_Q + (hd + 1) * HEAD_DIM]
        blk = _rope_rows(_rms_rows(blk, qg_ref[...]), cos, sin) * attn_scale
        qt_ref[hd * HEAD_DIM:(hd + 1) * HEAD_DIM, :] = blk.astype(bf16)
        q_hi, q_mid, q_low = _split3(_rope_rows(pti[I_Q + hd * IDX_DIM:I_Q + (hd + 1) * IDX_DIM], cos, sin))
        qit_ref[hd * IDX_K:(hd + 1) * IDX_K, :] = jnp.concatenate(
            [q_hi, q_mid, q_hi, q_low, q_hi, q_mid], axis=0).astype(bf16)
    kt = _rope_rows(_rms_rows(pt[T_K:T_V], kg_ref[...]), cos, sin)
    k_hi, k_mid, k_low = _split3(_rope_rows(pti[I_K:I_W], cos, sin))
    kk = jnp.concatenate([k_hi, k_hi, k_mid, k_hi, k_low, k_mid, kt, jnp.zeros_like(kt)], axis=0)
    kk = kk.T.astype(bf16)
    row = lax.broadcasted_iota(i32, (V_ROWS - HEAD_DIM, CHUNK), 0)
    ones_rows = jnp.where(row == 0, 1.0, 0.0).astype(bf16)
    for j in range(TILE // CHUNK):
        c = i * (TILE // CHUNK) + j
        ki_ref[c] = kk[j * CHUNK:(j + 1) * CHUNK, 0:IDX_K]
        k_ref[c] = kk[j * CHUNK:(j + 1) * CHUNK, IDX_K:IDX_K + HEAD_DIM]
        vt_ref[c, 0:HEAD_DIM, :] = pt[T_V:T_END, j * CHUNK:(j + 1) * CHUNK].astype(bf16)
        vt_ref[c, HEAD_DIM:V_ROWS, :] = ones_rows
    idx_w_scale = (IDX_HEADS ** -0.5) * (IDX_DIM ** -0.5)
    widx_ref[...] = pti[I_W:I_W + IDX_HEADS] * idx_w_scale

    key_row = lax.broadcasted_iota(i32, (CHUNK, TILE), 0)
    q_pos = q0 + lax.broadcasted_iota(i32, (CHUNK, TILE), 1)

    def score_dots(c, buf_ref):
        kic = ki_ref[c]
        for hd in range(IDX_HEADS):
            buf_ref[hd] = jnp.dot(kic, qit_ref[hd * IDX_K:(hd + 1) * IDX_K, :],
                                  preferred_element_type=f32)
        return jnp.int32(0)

    def score_keys(c, buf_ref, _):
        score = jnp.zeros((CHUNK, TILE), f32)
        for hd in range(IDX_HEADS):
            score = score + jnp.maximum(buf_ref[hd], 0.0) * widx_ref[hd:hd + 1, :]
        causal = (c * CHUNK + key_row) <= q_pos
        score = jnp.where(causal, score, -jnp.inf)
        score = jnp.where(score == 0.0, 0.0, score)
        bits = pltpu.bitcast(score, i32)
        key = bits ^ ((bits >> 31) & jnp.int32(0x7FFFFFFF))
        key_ref[c] = key
        half_ref[c] = (key >> HALF_BITS).astype(i16)

    _pipelined_chunks(nch, score_dots, score_keys, lta_ref, ltb_ref)

    k_f = jnp.full((1, TILE), float(top_k), f32)

    def search(n_chunks):
        def count_ge(mid):
            mid16 = mid.astype(i16)
            acc = None
            for c in range(n_chunks):
                one = jnp.where(half_ref[c] >= mid16, jnp.bfloat16(1), jnp.bfloat16(0))
                parts = [one[s * 32:(s + 1) * 32] for s in range(CHUNK // 32)]
                while len(parts) > 1:
                    parts = [parts[j] + parts[j + 1] for j in range(0, len(parts), 2)]
                acc = parts[0] if acc is None else acc + parts[0]
            return jnp.sum(acc.astype(f32), axis=0, keepdims=True)

        def bisect_half(wanted, cnt_lo0, proj_steps):
            def step(carry):
                lo, hi, cnt_lo, cnt_hi = carry
                mid = (lo + hi) >> 1
                cnt = count_ge(mid)
                ge = cnt >= wanted
                return (jnp.where(ge, mid, lo), jnp.where(ge, hi, mid),
                        jnp.where(ge, cnt, cnt_lo), jnp.where(ge, cnt_hi, cnt))

            def step_with_proj(t, carry):
                for r in range(2):
                    rows = slice(r * TILE // 2, (r + 1) * TILE // 2)
                    pn_ref[t, rows, :] = jnp.dot(hb_ref[rows, :], wn_ref[t], preferred_element_type=f32)
                return step(carry)

            lo0 = jnp.full((1, TILE), HALF_MIN, i32)
            hi0 = jnp.full((1, TILE), HALF_MAX + 1, i32)
            carry = (lo0, hi0, cnt_lo0, jnp.zeros((1, TILE), f32))
            if proj_steps:
                for t in range(HALF_BITS):
                    carry = step_with_proj(t, carry) if t < proj_steps else step(carry)
            else:
                carry = lax.fori_loop(0, HALF_BITS, lambda _, cr: step(cr), carry)
            lo, _, cnt_lo, cnt_hi = carry
            return lo, cnt_lo, cnt_hi

        processed = jnp.full((1, TILE), float(n_chunks * CHUNK), f32)
        thr_hi, cnt_ge_hi, cnt_gt_hi = bisect_half(k_f, processed, N_SLABS)
        for c in range(n_chunks):
            key = key_ref[c]
            low = (key & jnp.int32(HALF_MAX - HALF_MIN)) + HALF_MIN
            in_bucket = (key >> HALF_BITS) == thr_hi
            half_ref[c] = jnp.where(in_bucket, low, HALF_MIN).astype(i16)
        thr_lo, cnt_ge_lo, cnt_gt_lo = bisect_half(k_f - cnt_gt_hi, cnt_ge_hi - cnt_gt_hi, 0)
        thr_ref[...] = (thr_hi << HALF_BITS) + (thr_lo - HALF_MIN)
        cnt_ref[0:1, :] = cnt_gt_hi + cnt_gt_lo
        cnt_ref[1:2, :] = cnt_gt_hi + cnt_ge_lo

    for tile_idx in range(seq_len // TILE):
        pl.when(i == tile_idx)(functools.partial(search, (tile_idx + 1) * TILE // CHUNK))
    thr = thr_ref[...]
    cnt_gt = cnt_ref[0:1, :]
    cnt_ge = cnt_ref[1:2, :]
    ties_wanted = k_f - cnt_gt
    any_ties = jnp.max(cnt_ge - k_f) > 0.0

    def write_bias(c, sel):
        causal = (c * CHUNK + key_row) <= q_pos
        bias_ref[c] = jnp.where(sel & causal, 0.0, -jnp.inf)

    @pl.when(jnp.logical_not(any_ties))
    def _():
        def body(c, carry):
            write_bias(c, key_ref[c] >= thr)
            return carry
        lax.fori_loop(0, nch, body, 0)

    @pl.when(any_ties)
    def _():
        lower_tri = (lax.broadcasted_iota(i32, (CHUNK, CHUNK), 1)
                     <= lax.broadcasted_iota(i32, (CHUNK, CHUNK), 0))
        lower_tri = jnp.where(lower_tri, 1.0, 0.0).astype(bf16)

        def body(c, seen):
            key = key_ref[c]
            tie = key == thr
            tie_f = jnp.where(tie, 1.0, 0.0)
            rank = jnp.dot(lower_tri, tie_f.astype(bf16), preferred_element_type=f32) + seen
            write_bias(c, (key > thr) | (tie & (rank <= ties_wanted)))
            return seen + jnp.sum(tie_f, axis=0, keepdims=True)
        lax.fori_loop(0, nch, body, jnp.zeros((1, TILE), f32))

    m_ref[...] = jnp.full((N_HEADS, TILE), NEG_BIG, f32)
    acc_ref[...] = jnp.zeros((N_HEADS, V_ROWS, TILE), f32)

    def attn_logits(c, buf_ref):
        kc = k_ref[c]
        bias = bias_ref[c]
        col_max = []
        for hd in range(N_HEADS):
            lt = jnp.dot(kc, qt_ref[hd * HEAD_DIM:(hd + 1) * HEAD_DIM, :],
                         preferred_element_type=f32) + bias
            buf_ref[hd] = lt
            col_max.append(jnp.max(lt, axis=0, keepdims=True))
        return tuple(col_max)

    def attn_accumulate(c, buf_ref, col_max):
        vc = vt_ref[c]
        for hd in range(N_HEADS):
            m_old = m_ref[hd:hd + 1, :]
            m_new = jnp.maximum(m_old, col_max[hd])
            p = jnp.exp2(buf_ref[hd] - m_new).astype(bf16)
            alpha = jnp.exp2(m_old - m_new)
            acc_ref[hd] = acc_ref[hd] * alpha + jnp.dot(vc, p, preferred_element_type=f32)
            m_ref[hd:hd + 1, :] = m_new

    _pipelined_chunks(nch, attn_logits, attn_accumulate, lta_ref, ltb_ref)
    for hd in range(N_HEADS):
        a = acc_ref[hd]
        attn_ref[hd * HEAD_DIM:(hd + 1) * HEAD_DIM, :] = a[0:HEAD_DIM] / a[HEAD_DIM:HEAD_DIM + 1]
    attn = attn_ref[...].T

    def proj_cols(start, stop):
        return jnp.concatenate([pn_ref[s] for s in range(start // SLAB, stop // SLAB)], axis=1)

    u_a = proj_cols(N_UA, N_ZA)

    @pl.when(i == 0)
    def _():
        ext_ref[0:MAX_WINDOW, :] = jnp.zeros((MAX_WINDOW, POOL_WIDTH), f32)

    @pl.when(i > 0)
    def _():
        ext_ref[0:MAX_WINDOW, :] = ext_ref[TILE:TILE + MAX_WINDOW, :]

    ext_ref[MAX_WINDOW:MAX_WINDOW + TILE, :] = u_a
    t_pos = q0 + lax.broadcasted_iota(i32, (TILE, 1), 0)
    mixed = []
    for g, w in enumerate(POOL_WINDOWS):
        cols = slice(g * POOL_GROUP_DIM, (g + 1) * POOL_GROUP_DIM)
        win = ext_ref[MAX_WINDOW:MAX_WINDOW + TILE, cols]
        for j in range(1, w):
            win = win + ext_ref[MAX_WINDOW - j:MAX_WINDOW - j + TILE, cols]
        count = jnp.minimum(t_pos + 1, w).astype(f32)
        pooled = win / count - u_a[:, cols]
        mixed.append(jnp.dot(pooled.astype(bf16), wpool_ref[g], preferred_element_type=f32))
    mixed = jnp.concatenate(mixed, axis=1) * pscale_ref[...]
    z_a = proj_cols(N_ZA, N_ZB)
    y_a = mixed * (z_a * jax.nn.sigmoid(z_a))
    a_out = jnp.dot(y_a.astype(bf16), wa_ref[...], preferred_element_type=f32)

    z_b = proj_cols(N_ZB, N_GA)
    y_b = attn * (z_b * jax.nn.sigmoid(z_b))
    b_out = jnp.dot(y_b.astype(bf16), wb_ref[...], preferred_element_type=f32)
    g_a = proj_cols(N_GA, N_GB)
    g_b = proj_cols(N_GB, N_END)
    merged = jax.nn.sigmoid(g_a) * a_out + jax.nn.sigmoid(g_b) * b_out
    y = jnp.dot(merged.astype(bf16), wo_ref[...], preferred_element_type=f32)
    out_ref[0] = x + gate * y


def _const_spec(shape):
    zeros = (0,) * len(shape)
    return pl.BlockSpec(shape, lambda b, i: zeros, pipeline_mode=pl.Buffered(1))


def _layer(x, mod3, pos3, gain, invf, w_n, w_t, w_ti, w_pool, pscale, qg, kg, w_a, w_b, w_o):
    batch, seq_len, d = x.shape
    assert d == D_MODEL and seq_len % TILE == 0 and TILE % CHUNK == 0
    assert seq_len // 32 <= 256, "bf16 partial counts must stay exactly representable"
    n_chunks = seq_len // CHUNK
    top_k = min(TOPK_MAX, seq_len // 4)
    bf16, f32, i32 = jnp.bfloat16, jnp.float32, jnp.int32
    kernel = functools.partial(_layer_kernel, seq_len=seq_len, top_k=top_k)
    return pl.pallas_call(
        kernel,
        grid=(batch, seq_len // TILE),
        in_specs=[
            pl.BlockSpec((1, TILE, D_MODEL), lambda b, i: (b, i, 0)),
            pl.BlockSpec((1, 3, D_MODEL), lambda b, i: (b, 0, 0)),
            pl.BlockSpec((1, 1, TILE), lambda b, i: (b, 0, i)),
            _const_spec(gain.shape), _const_spec(invf.shape), _const_spec(w_n.shape),
            _const_spec(w_t.shape), _const_spec(w_ti.shape), _const_spec(w_pool.shape), _const_spec(pscale.shape),
            _const_spec(qg.shape), _const_spec(kg.shape), _const_spec(w_a.shape),
            _const_spec(w_b.shape), _const_spec(w_o.shape),
        ],
        out_specs=pl.BlockSpec((1, TILE, D_MODEL), lambda b, i: (b, i, 0)),
        out_shape=jax.ShapeDtypeStruct(x.shape, x.dtype),
        scratch_shapes=[
            pltpu.VMEM((TILE + MAX_WINDOW, POOL_WIDTH), f32),
            pltpu.VMEM((n_chunks, CHUNK, HEAD_DIM), bf16),
            pltpu.VMEM((n_chunks, CHUNK, IDX_K), bf16),
            pltpu.VMEM((n_chunks, V_ROWS, CHUNK), bf16),
            pltpu.VMEM((ATTN_WIDTH, TILE), bf16),
            pltpu.VMEM((IDX_HEADS * IDX_K, TILE), bf16),
            pltpu.VMEM((IDX_HEADS, TILE), f32),
            pltpu.VMEM((n_chunks, CHUNK, TILE), i32),
            pltpu.VMEM((n_chunks, CHUNK, TILE), f32),
            pltpu.VMEM((N_HEADS, TILE), f32),
            pltpu.VMEM((N_HEADS, V_ROWS, TILE), f32),
            pltpu.VMEM((ATTN_WIDTH, TILE), f32),
            pltpu.VMEM((N_HEADS, CHUNK, TILE), f32),
            pltpu.VMEM((N_HEADS, CHUNK, TILE), f32),
            pltpu.VMEM((n_chunks, CHUNK, TILE), jnp.int16),
            pltpu.VMEM((TILE, D_MODEL), bf16),
            pltpu.VMEM((N_SLABS, TILE, SLAB), f32),
            pltpu.VMEM((1, TILE), i32),
            pltpu.VMEM((2, TILE), f32),
        ],
        compiler_params=pltpu.CompilerParams(
            dimension_semantics=("arbitrary", "arbitrary"),
            vmem_limit_bytes=VMEM_LIMIT_BYTES),
        name="hybrid_layer",
    )(x, mod3, pos3, gain, invf, w_n, w_t, w_ti, w_pool, pscale, qg, kg, w_a, w_b, w_o)


def _modulation(c, w_ada, b_ada):
    batch, d = c.shape
    n_out = w_ada.shape[1]
    blk = D_MODEL
    return pl.pallas_call(
        _mod_kernel,
        grid=(n_out // blk,),
        in_specs=[pl.BlockSpec((batch, d), lambda j: (0, 0)),
                  pl.BlockSpec((d, blk), lambda j: (0, j)),
                  pl.BlockSpec((1, blk), lambda j: (0, j))],
        out_specs=pl.BlockSpec((batch, blk), lambda j: (0, j)),
        out_shape=jax.ShapeDtypeStruct((batch, n_out), jnp.float32),
        name="adaln_mod",
    )(c, w_ada, b_ada.reshape(1, n_out))


def kernel(x, c, positions, w_ada, b_ada, norm_gain, w_in, w_pool_group, pool_scale,
           q_norm_gain, k_norm_gain, w_branch_a, w_branch_b, w_out):
    batch, seq_len, _ = x.shape
    bf16 = jnp.bfloat16
    inv_freq = ROPE_THETA ** (-jnp.arange(0, ROT_DIM, 2, dtype=jnp.float32) / ROT_DIM)
    invf = inv_freq.reshape(ROT_HALF, 1)
    pos3 = positions.reshape(batch, 1, seq_len)
    bounds = [0]
    for s in IN_SIZES:
        bounds.append(bounds[-1] + s)
    depth = w_in.shape[0]
    for layer in range(depth):
        seg = [w_in[layer][:, bounds[j]:bounds[j + 1]] for j in range(len(IN_SIZES))]
        u_a, z_a, q, k, v, z_b, q_idx, k_idx, w_idx, g_a, g_b = seg
        w_n = jnp.concatenate([u_a, z_a, z_b, g_a, g_b], axis=1).astype(bf16)
        w_n = w_n.reshape(D_MODEL, N_SLABS, SLAB).transpose(1, 0, 2)
        w_t = jnp.concatenate([q, k, v], axis=1).T.astype(bf16)
        pad = jnp.zeros((D_MODEL, I_END - I_W - IDX_HEADS), jnp.float32)
        w_ti = jnp.concatenate(_split3(jnp.concatenate([q_idx, k_idx, w_idx, pad], axis=1).T),
                               axis=0).astype(bf16)
        mod = _modulation(c, w_ada[layer], b_ada[layer])
        x = _layer(
            x, mod.reshape(batch, 3, D_MODEL), pos3, norm_gain[layer].reshape(1, D_MODEL), invf,
            w_n, w_t, w_ti, w_pool_group[layer].astype(bf16), pool_scale[layer].reshape(1, POOL_WIDTH),
            q_norm_gain[layer].reshape(HEAD_DIM, 1), k_norm_gain[layer].reshape(HEAD_DIM, 1),
            w_branch_a[layer].astype(bf16), w_branch_b[layer].astype(bf16), w_out[layer].astype(bf16))
    return x
```

```python
import functools

import jax
import jax.numpy as jnp
from jax import lax
from jax.experimental import pallas as pl
from jax.experimental.pallas import tpu as pltpu

D_MODEL = 1024
POOL_WIDTH = 512
POOL_GROUPS = 4
POOL_GROUP_DIM = 128
POOL_WINDOWS = (2, 4, 8, 16)
MAX_WINDOW = 16
N_HEADS = 8
HEAD_DIM = 64
ATTN_WIDTH = N_HEADS * HEAD_DIM
IDX_HEADS = 8
IDX_DIM = 64
TOPK_MAX = 256
ROPE_THETA = 500000.0
ROT_DIM = HEAD_DIM // 4
ROT_HALF = ROT_DIM // 2
EPS = 1e-6
IN_SIZES = (POOL_WIDTH, POOL_WIDTH, ATTN_WIDTH, HEAD_DIM, HEAD_DIM, ATTN_WIDTH,
            IDX_HEADS * IDX_DIM, IDX_DIM, IDX_HEADS, D_MODEL, D_MODEL)

TILE = 256
CHUNK = 256
V_ROWS = 80
NEG_BIG = -1e30
LOG2_E = 1.4426950408889634
HALF_BITS = 16
HALF_MIN, HALF_MAX = -(1 << 15), (1 << 15) - 1
VMEM_LIMIT_BYTES = 56 * 1024 * 1024

T_Q, T_K, T_V, T_END = 0, 512, 576, 640
I_Q, I_K, I_W, I_END = 0, 512, 576, 592
IDX_K = 6 * IDX_DIM
N_UA, N_ZA, N_ZB, N_GA, N_GB, N_END = 0, 512, 1024, 1536, 2560, 3584
SLAB = 256
N_SLABS = N_END // SLAB
assert N_SLABS <= HALF_BITS

_NT_DIMS = (((1,), (1,)), ((), ()))


def _mod_kernel(c_ref, w_ref, b_ref, o_ref):
    c = c_ref[...]
    s = c * jax.nn.sigmoid(c)
    o_ref[...] = jnp.dot(s, w_ref[...], preferred_element_type=jnp.float32,
                         precision=lax.Precision.HIGHEST) + b_ref[...]


def _rope_rows(blk, cos, sin):
    x1 = blk[0:ROT_HALF]
    x2 = blk[ROT_HALF:ROT_DIM]
    return jnp.concatenate([x1 * cos - x2 * sin, x2 * cos + x1 * sin, blk[ROT_DIM:]], axis=0)


def _split3(x):
    hi = x.astype(jnp.bfloat16).astype(jnp.float32)
    rest = x - hi
    mid = rest.astype(jnp.bfloat16).astype(jnp.float32)
    low = (rest - mid).astype(jnp.bfloat16).astype(jnp.float32)
    return hi, mid, low


def _rms_rows(blk, gain_col):
    ms = jnp.mean(blk * blk, axis=0, keepdims=True)
    return blk * lax.rsqrt(ms + EPS) * gain_col


def _layer_kernel(x_ref, mod_ref, pos_ref, gain_ref, invf_ref, wn_ref, wt_ref, wti_ref, wpool_ref,
                  pscale_ref, qg_ref, kg_ref, wa_ref, wb_ref, wo_ref, out_ref,
                  ext_ref, k_ref, ki_ref, vt_ref, qt_ref, qit_ref, widx_ref, key_ref, bias_ref,
                  m_ref, acc_ref, attn_ref, lta_ref, ltb_ref, half_ref, hb_ref, pn_ref, thr_ref,
                  cnt_ref, *, seq_len, top_k):
    i = pl.program_id(1)
    q0 = i * TILE
    nch = (q0 + TILE) // CHUNK
    f32, bf16, i32, i16 = jnp.float32, jnp.bfloat16, jnp.int32, jnp.int16

    x = x_ref[0]
    shift = mod_ref[0, 0:1, :]
    scale = mod_ref[0, 1:2, :]
    gate = mod_ref[0, 2:3, :]
    ms = jnp.mean(x * x, axis=-1, keepdims=True)
    h = (x * lax.rsqrt(ms + EPS) * gain_ref[...]) * (1.0 + scale) + shift
    hb = h.astype(bf16)
    hb_ref[...] = hb

    pt = lax.dot_general(wt_ref[...], hb, _NT_DIMS, preferred_element_type=f32)
    h_mid_f = h - hb.astype(f32)
    h_mid = h_mid_f.astype(bf16)
    h_low = (h_mid_f - h_mid.astype(f32)).astype(bf16)
    p_hi = lax.dot_general(wti_ref[...], hb, _NT_DIMS, preferred_element_type=f32)
    p_mid = lax.dot_general(wti_ref[0:2 * I_END, :], h_mid, _NT_DIMS, preferred_element_type=f32)
    p_low = lax.dot_general(wti_ref[0:I_END, :], h_low, _NT_DIMS, preferred_element_type=f32)
    pti = (((p_mid[I_END:2 * I_END] + p_hi[2 * I_END:3 * I_END]) + p_low)
           + (p_mid[0:I_END] + p_hi[I_END:2 * I_END])) + p_hi[0:I_END]
    ang = invf_ref[...] * pos_ref[0].astype(f32)
    cos = jnp.cos(ang)
    sin = jnp.sin(ang)
    attn_scale = HEAD_DIM ** -0.5 * LOG2_E
    for hd in range(N_HEADS):
        blk = pt[T_Q + hd * HEAD_DIM:T_Q + (hd + 1) * HEAD_DIM]
        blk = _rope_rows(_rms_rows(blk, qg_ref[...]), cos, sin) * attn_scale
        qt_ref[hd * HEAD_DIM:(hd + 1) * HEAD_DIM, :] = blk.astype(bf16)
        q_hi, q_mid, q_low = _split3(_rope_rows(pti[I_Q + hd * IDX_DIM:I_Q + (hd + 1) * IDX_DIM], cos, sin))
        qit_ref[hd * IDX_K:(hd + 1) * IDX_K, :] = jnp.concatenate(
            [q_hi, q_mid, q_hi, q_low, q_hi, q_mid], axis=0).astype(bf16)
    kt = _rope_rows(_rms_rows(pt[T_K:T_V], kg_ref[...]), cos, sin)
    k_hi, k_mid, k_low = _split3(_rope_rows(pti[I_K:I_W], cos, sin))
    kk = jnp.concatenate([k_hi, k_hi, k_mid, k_hi, k_low, k_mid, kt, jnp.zeros_like(kt)], axis=0)
    kk = kk.T.astype(bf16)
    row = lax.broadcasted_iota(i32, (V_ROWS - HEAD_DIM, CHUNK), 0)
    ones_rows = jnp.where(row == 0, 1.0, 0.0).astype(bf16)
    for j in range(TILE // CHUNK):
        c = i * (TILE // CHUNK) + j
        ki_ref[c] = kk[j * CHUNK:(j + 1) * CHUNK, 0:IDX_K]
        k_ref[c] = kk[j * CHUNK:(j + 1) * CHUNK, IDX_K:IDX_K + HEAD_DIM]
        vt_ref[c, 0:HEAD_DIM, :] = pt[T_V:T_END, j * CHUNK:(j + 1) * CHUNK].astype(bf16)
        vt_ref[c, HEAD_DIM:V_ROWS, :] = ones_rows
    idx_w_scale = (IDX_HEADS ** -0.5) * (IDX_DIM ** -0.5)
    widx_ref[...] = pti[I_W:I_W + IDX_HEADS] * idx_w_scale

    key_row = lax.broadcasted_iota(i32, (CHUNK, TILE), 0)
    q_pos = q0 + lax.broadcasted_iota(i32, (CHUNK, TILE), 1)

    def score_chunk(c):
        kic = ki_ref[c]
        score = jnp.zeros((CHUNK, TILE), f32)
        for hd in range(IDX_HEADS):
            s = jnp.dot(kic, qit_ref[hd * IDX_K:(hd + 1) * IDX_K, :], preferred_element_type=f32)
            score = score + jnp.maximum(s, 0.0) * widx_ref[hd:hd + 1, :]
        causal = (c * CHUNK + key_row) <= q_pos
        score = jnp.where(causal, score, -jnp.inf)
        score = jnp.where(score == 0.0, 0.0, score)
        bits = pltpu.bitcast(score, i32)
        key = bits ^ ((bits >> 31) & jnp.int32(0x7FFFFFFF))
        key_ref[c] = key
        half_ref[c] = (key >> HALF_BITS).astype(i16)

    def score_pair(p, carry):
        score_chunk(2 * p)
        score_chunk(2 * p + 1)
        return carry

    lax.fori_loop(0, nch // 2, score_pair, 0)

    @pl.when(nch % 2 == 1)
    def _():
        score_chunk(nch - 1)

    k_f = jnp.full((1, TILE), float(top_k), f32)

    def search(n_chunks):
        def count_ge(mid):
            mid16 = mid.astype(i16)
            acc = None
            for c in range(n_chunks):
                one = jnp.where(half_ref[c] >= mid16, jnp.bfloat16(1), jnp.bfloat16(0))
                parts = [one[s * 32:(s + 1) * 32] for s in range(CHUNK // 32)]
                while len(parts) > 1:
                    parts = [parts[j] + parts[j + 1] for j in range(0, len(parts), 2)]
                acc = parts[0] if acc is None else acc + parts[0]
            return jnp.sum(acc.astype(f32), axis=0, keepdims=True)

        def bisect_half(wanted, cnt_lo0, proj_steps):
            def step(carry):
                lo, hi, cnt_lo, cnt_hi = carry
                mid = (lo + hi) >> 1
                cnt = count_ge(mid)
                ge = cnt >= wanted
                return (jnp.where(ge, mid, lo), jnp.where(ge, hi, mid),
                        jnp.where(ge, cnt, cnt_lo), jnp.where(ge, cnt_hi, cnt))

            def step_with_proj(t, carry):
                for r in range(2):
                    rows = slice(r * TILE // 2, (r + 1) * TILE // 2)
                    pn_ref[t, rows, :] = jnp.dot(hb_ref[rows, :], wn_ref[t], preferred_element_type=f32)
                return step(carry)

            lo0 = jnp.full((1, TILE), HALF_MIN, i32)
            hi0 = jnp.full((1, TILE), HALF_MAX + 1, i32)
            carry = (lo0, hi0, cnt_lo0, jnp.zeros((1, TILE), f32))
            if proj_steps:
                for t in range(HALF_BITS):
                    carry = step_with_proj(t, carry) if t < proj_steps else step(carry)
            else:
                carry = lax.fori_loop(0, HALF_BITS, lambda _, cr: step(cr), carry)
            lo, _, cnt_lo, cnt_hi = carry
            return lo, cnt_lo, cnt_hi

        processed = jnp.full((1, TILE), float(n_chunks * CHUNK), f32)
        thr_hi, cnt_ge_hi, cnt_gt_hi = bisect_half(k_f, processed, N_SLABS)
        for c in range(n_chunks):
            key = key_ref[c]
            low = (key & jnp.int32(HALF_MAX - HALF_MIN)) + HALF_MIN
            in_bucket = (key >> HALF_BITS) == thr_hi
            half_ref[c] = jnp.where(in_bucket, low, HALF_MIN).astype(i16)
        thr_lo, cnt_ge_lo, cnt_gt_lo = bisect_half(k_f - cnt_gt_hi, cnt_ge_hi - cnt_gt_hi, 0)
        thr_ref[...] = (thr_hi << HALF_BITS) + (thr_lo - HALF_MIN)
        cnt_ref[0:1, :] = cnt_gt_hi + cnt_gt_lo
        cnt_ref[1:2, :] = cnt_gt_hi + cnt_ge_lo

    for tile_idx in range(seq_len // TILE):
        pl.when(i == tile_idx)(functools.partial(search, (tile_idx + 1) * TILE // CHUNK))
    thr = thr_ref[...]
    cnt_gt = cnt_ref[0:1, :]
    cnt_ge = cnt_ref[1:2, :]
    ties_wanted = k_f - cnt_gt
    any_ties = jnp.max(cnt_ge - k_f) > 0.0

    def write_bias(c, sel):
        causal = (c * CHUNK + key_row) <= q_pos
        bias_ref[c] = jnp.where(sel & causal, 0.0, -jnp.inf)

    @pl.when(jnp.logical_not(any_ties))
    def _():
        def body(c, carry):
            write_bias(c, key_ref[c] >= thr)
            return carry
        lax.fori_loop(0, nch, body, 0)

    @pl.when(any_ties)
    def _():
        lower_tri = (lax.broadcasted_iota(i32, (CHUNK, CHUNK), 1)
                     <= lax.broadcasted_iota(i32, (CHUNK, CHUNK), 0))
        lower_tri = jnp.where(lower_tri, 1.0, 0.0).astype(bf16)

        def body(c, seen):
            key = key_ref[c]
            tie = key == thr
            tie_f = jnp.where(tie, 1.0, 0.0)
            rank = jnp.dot(lower_tri, tie_f.astype(bf16), preferred_element_type=f32) + seen
            write_bias(c, (key > thr) | (tie & (rank <= ties_wanted)))
            return seen + jnp.sum(tie_f, axis=0, keepdims=True)
        lax.fori_loop(0, nch, body, jnp.zeros((1, TILE), f32))

    m_ref[...] = jnp.full((N_HEADS, TILE), NEG_BIG, f32)
    acc_ref[...] = jnp.zeros((N_HEADS, V_ROWS, TILE), f32)

    def attn_logits(c, buf_ref):
        kc = k_ref[c]
        bias = bias_ref[c]
        col_max = []
        for hd in range(N_HEADS):
            lt = jnp.dot(kc, qt_ref[hd * HEAD_DIM:(hd + 1) * HEAD_DIM, :],
                         preferred_element_type=f32) + bias
            buf_ref[hd] = lt
            col_max.append(jnp.max(lt, axis=0, keepdims=True))
        return tuple(col_max)

    def attn_accumulate(c, buf_ref, col_max):
        vc = vt_ref[c]
        for hd in range(N_HEADS):
            m_old = m_ref[hd:hd + 1, :]
            m_new = jnp.maximum(m_old, col_max[hd])
            p = jnp.exp2(buf_ref[hd] - m_new).astype(bf16)
            alpha = jnp.exp2(m_old - m_new)
            acc_ref[hd] = acc_ref[hd] * alpha + jnp.dot(vc, p, preferred_element_type=f32)
            m_ref[hd:hd + 1, :] = m_new

    def attn_pair(p, max_a):
        c = 2 * p
        max_b = attn_logits(c + 1, ltb_ref)
        attn_accumulate(c, lta_ref, max_a)
        max_a = attn_logits(c + 2, lta_ref)
        attn_accumulate(c + 1, ltb_ref, max_b)
        return max_a

    n_pairs = (nch - 1) // 2
    max_a = lax.fori_loop(0, n_pairs, attn_pair, attn_logits(0, lta_ref))
    c_last = 2 * n_pairs

    @pl.when(c_last == nch - 1)
    def _():
        attn_accumulate(c_last, lta_ref, max_a)

    @pl.when(c_last != nch - 1)
    def _():
        max_b = attn_logits(c_last + 1, ltb_ref)
        attn_accumulate(c_last, lta_ref, max_a)
        attn_accumulate(c_last + 1, ltb_ref, max_b)
    for hd in range(N_HEADS):
        a = acc_ref[hd]
        attn_ref[hd * HEAD_DIM:(hd + 1) * HEAD_DIM, :] = a[0:HEAD_DIM] / a[HEAD_DIM:HEAD_DIM + 1]
    attn = attn_ref[...].T

    def proj_cols(start, stop):
        return jnp.concatenate([pn_ref[s] for s in range(start // SLAB, stop // SLAB)], axis=1)

    u_a = proj_cols(N_UA, N_ZA)

    @pl.when(i == 0)
    def _():
        ext_ref[0:MAX_WINDOW, :] = jnp.zeros((MAX_WINDOW, POOL_WIDTH), f32)

    @pl.when(i > 0)
    def _():
        ext_ref[0:MAX_WINDOW, :] = ext_ref[TILE:TILE + MAX_WINDOW, :]

    ext_ref[MAX_WINDOW:MAX_WINDOW + TILE, :] = u_a
    t_pos = q0 + lax.broadcasted_iota(i32, (TILE, 1), 0)
    mixed = []
    for g, w in enumerate(POOL_WINDOWS):
        cols = slice(g * POOL_GROUP_DIM, (g + 1) * POOL_GROUP_DIM)
        win = ext_ref[MAX_WINDOW:MAX_WINDOW + TILE, cols]
        for j in range(1, w):
            win = win + ext_ref[MAX_WINDOW - j:MAX_WINDOW - j + TILE, cols]
        count = jnp.minimum(t_pos + 1, w).astype(f32)
        pooled = win / count - u_a[:, cols]
        mixed.append(jnp.dot(pooled.astype(bf16), wpool_ref[g], preferred_element_type=f32))
    mixed = jnp.concatenate(mixed, axis=1) * pscale_ref[...]
    z_a = proj_cols(N_ZA, N_ZB)
    y_a = mixed * (z_a * jax.nn.sigmoid(z_a))
    a_out = jnp.dot(y_a.astype(bf16), wa_ref[...], preferred_element_type=f32)

    z_b = proj_cols(N_ZB, N_GA)
    y_b = attn * (z_b * jax.nn.sigmoid(z_b))
    b_out = jnp.dot(y_b.astype(bf16), wb_ref[...], preferred_element_type=f32)
    g_a = proj_cols(N_GA, N_GB)
    g_b = proj_cols(N_GB, N_END)
    merged = jax.nn.sigmoid(g_a) * a_out + jax.nn.sigmoid(g_b) * b_out
    y = jnp.dot(merged.astype(bf16), wo_ref[...], preferred_element_type=f32)
    out_ref[0] = x + gate * y


def _const_spec(shape):
    zeros = (0,) * len(shape)
    return pl.BlockSpec(shape, lambda b, i: zeros, pipeline_mode=pl.Buffered(1))


def _layer(x, mod3, pos3, gain, invf, w_n, w_t, w_ti, w_pool, pscale, qg, kg, w_a, w_b, w_o):
    batch, seq_len, d = x.shape
    assert d == D_MODEL and seq_len % TILE == 0 and TILE % CHUNK == 0
    assert seq_len // 32 <= 256, "bf16 partial counts must stay exactly representable"
    n_chunks = seq_len // CHUNK
    top_k = min(TOPK_MAX, seq_len // 4)
    bf16, f32, i32 = jnp.bfloat16, jnp.float32, jnp.int32
    kernel = functools.partial(_layer_kernel, seq_len=seq_len, top_k=top_k)
    return pl.pallas_call(
        kernel,
        grid=(batch, seq_len // TILE),
        in_specs=[
            pl.BlockSpec((1, TILE, D_MODEL), lambda b, i: (b, i, 0)),
            pl.BlockSpec((1, 3, D_MODEL), lambda b, i: (b, 0, 0)),
            pl.BlockSpec((1, 1, TILE), lambda b, i: (b, 0, i)),
            _const_spec(gain.shape), _const_spec(invf.shape), _const_spec(w_n.shape),
            _const_spec(w_t.shape), _const_spec(w_ti.shape), _const_spec(w_pool.shape), _const_spec(pscale.shape),
            _const_spec(qg.shape), _const_spec(kg.shape), _const_spec(w_a.shape),
            _const_spec(w_b.shape), _const_spec(w_o.shape),
        ],
        out_specs=pl.BlockSpec((1, TILE, D_MODEL), lambda b, i: (b, i, 0)),
        out_shape=jax.ShapeDtypeStruct(x.shape, x.dtype),
        scratch_shapes=[
            pltpu.VMEM((TILE + MAX_WINDOW, POOL_WIDTH), f32),
            pltpu.VMEM((n_chunks, CHUNK, HEAD_DIM), bf16),
            pltpu.VMEM((n_chunks, CHUNK, IDX_K), bf16),
            pltpu.VMEM((n_chunks, V_ROWS, CHUNK), bf16),
            pltpu.VMEM((ATTN_WIDTH, TILE), bf16),
            pltpu.VMEM((IDX_HEADS * IDX_K, TILE), bf16),
            pltpu.VMEM((IDX_HEADS, TILE), f32),
            pltpu.VMEM((n_chunks, CHUNK, TILE), i32),
            pltpu.VMEM((n_chunks, CHUNK, TILE), f32),
            pltpu.VMEM((N_HEADS, TILE), f32),
            pltpu.VMEM((N_HEADS, V_ROWS, TILE), f32),
            pltpu.VMEM((ATTN_WIDTH, TILE), f32),
            pltpu.VMEM((N_HEADS, CHUNK, TILE), f32),
            pltpu.VMEM((N_HEADS, CHUNK, TILE), f32),
            pltpu.VMEM((n_chunks, CHUNK, TILE), jnp.int16),
            pltpu.VMEM((TILE, D_MODEL), bf16),
            pltpu.VMEM((N_SLABS, TILE, SLAB), f32),
            pltpu.VMEM((1, TILE), i32),
            pltpu.VMEM((2, TILE), f32),
        ],
        compiler_params=pltpu.CompilerParams(
            dimension_semantics=("arbitrary", "arbitrary"),
            vmem_limit_bytes=VMEM_LIMIT_BYTES),
        name="hybrid_layer",
    )(x, mod3, pos3, gain, invf, w_n, w_t, w_ti, w_pool, pscale, qg, kg, w_a, w_b, w_o)


def _modulation(c, w_ada, b_ada):
    batch, d = c.shape
    n_out = w_ada.shape[1]
    blk = D_MODEL
    return pl.pallas_call(
        _mod_kernel,
        grid=(n_out // blk,),
        in_specs=[pl.BlockSpec((batch, d), lambda j: (0, 0)),
                  pl.BlockSpec((d, blk), lambda j: (0, j)),
                  pl.BlockSpec((1, blk), lambda j: (0, j))],
        out_specs=pl.BlockSpec((batch, blk), lambda j: (0, j)),
        out_shape=jax.ShapeDtypeStruct((batch, n_out), jnp.float32),
        name="adaln_mod",
    )(c, w_ada, b_ada.reshape(1, n_out))


def kernel(x, c, positions, w_ada, b_ada, norm_gain, w_in, w_pool_group, pool_scale,
           q_norm_gain, k_norm_gain, w_branch_a, w_branch_b, w_out):
    batch, seq_len, _ = x.shape
    bf16 = jnp.bfloat16
    inv_freq = ROPE_THETA ** (-jnp.arange(0, ROT_DIM, 2, dtype=jnp.float32) / ROT_DIM)
    invf = inv_freq.reshape(ROT_HALF, 1)
    pos3 = positions.reshape(batch, 1, seq_len)
    bounds = [0]
    for s in IN_SIZES:
        bounds.append(bounds[-1] + s)
    depth = w_in.shape[0]
    for layer in range(depth):
        seg = [w_in[layer][:, bounds[j]:bounds[j + 1]] for j in range(len(IN_SIZES))]
        u_a, z_a, q, k, v, z_b, q_idx, k_idx, w_idx, g_a, g_b = seg
        w_n = jnp.concatenate([u_a, z_a, z_b, g_a, g_b], axis=1).astype(bf16)
        w_n = w_n.reshape(D_MODEL, N_SLABS, SLAB).transpose(1, 0, 2)
        w_t = jnp.concatenate([q, k, v], axis=1).T.astype(bf16)
        pad = jnp.zeros((D_MODEL, I_END - I_W - IDX_HEADS), jnp.float32)
        w_ti = jnp.concatenate(_split3(jnp.concatenate([q_idx, k_idx, w_idx, pad], axis=1).T),
                               axis=0).astype(bf16)
        mod = _modulation(c, w_ada[layer], b_ada[layer])
        x = _layer(
            x, mod.reshape(batch, 3, D_MODEL), pos3, norm_gain[layer].reshape(1, D_MODEL), invf,
            w_n, w_t, w_ti, w_pool_group[layer].astype(bf16), pool_scale[layer].reshape(1, POOL_WIDTH),
            q_norm_gain[layer].reshape(HEAD_DIM, 1), k_norm_gain[layer].reshape(HEAD_DIM, 1),
            w_branch_a[layer].astype(bf16), w_branch_b[layer].astype(bf16), w_out[layer].astype(bf16))
    return x
```

```python
import functools

import jax
import jax.numpy as jnp
from jax import lax
from jax.experimental import pallas as pl
from jax.experimental.pallas import tpu as pltpu

D_MODEL = 1024
POOL_WIDTH = 512
POOL_GROUPS = 4
POOL_GROUP_DIM = 128
POOL_WINDOWS = (2, 4, 8, 16)
MAX_WINDOW = 16
N_HEADS = 8
HEAD_DIM = 64
ATTN_WIDTH = N_HEADS * HEAD_DIM
IDX_HEADS = 8
IDX_DIM = 64
TOPK_MAX = 256
ROPE_THETA = 500000.0
ROT_DIM = HEAD_DIM // 4
ROT_HALF = ROT_DIM // 2
EPS = 1e-6
IN_SIZES = (POOL_WIDTH, POOL_WIDTH, ATTN_WIDTH, HEAD_DIM, HEAD_DIM, ATTN_WIDTH,
            IDX_HEADS * IDX_DIM, IDX_DIM, IDX_HEADS, D_MODEL, D_MODEL)

TILE = 256
CHUNK = 256
V_ROWS = 80
NEG_BIG = -1e30
LOG2_E = 1.4426950408889634
HALF_BITS = 16
HALF_MIN, HALF_MAX = -(1 << 15), (1 << 15) - 1
V7X_VMEM_BYTES = 64 * 1024 * 1024
VMEM_LIMIT_BYTES = V7X_VMEM_BYTES * 7 // 8

T_Q, T_K, T_V, T_END = 0, 512, 576, 640
I_Q, I_K, I_W, I_END = 0, 512, 576, 592
IDX_K = 6 * IDX_DIM
N_UA, N_ZA, N_ZB, N_GA, N_GB, N_END = 0, 512, 1024, 1536, 2560, 3584
SLAB = 256
N_SLABS = N_END // SLAB
assert N_SLABS <= HALF_BITS

_NT_DIMS = (((1,), (1,)), ((), ()))


def _mod_kernel(c_ref, w_ref, b_ref, o_ref):
    c = c_ref[...]
    s = c * jax.nn.sigmoid(c)
    o_ref[...] = jnp.dot(s, w_ref[...], preferred_element_type=jnp.float32,
                         precision=lax.Precision.HIGHEST) + b_ref[...]


def _rope_rows(blk, cos, sin):
    x1 = blk[0:ROT_HALF]
    x2 = blk[ROT_HALF:ROT_DIM]
    return jnp.concatenate([x1 * cos - x2 * sin, x2 * cos + x1 * sin, blk[ROT_DIM:]], axis=0)


def _split3(x):
    hi = x.astype(jnp.bfloat16).astype(jnp.float32)
    rest = x - hi
    mid = rest.astype(jnp.bfloat16).astype(jnp.float32)
    low = (rest - mid).astype(jnp.bfloat16).astype(jnp.float32)
    return hi, mid, low


def _rms_rows(blk, gain_col):
    ms = jnp.mean(blk * blk, axis=0, keepdims=True)
    return blk * lax.rsqrt(ms + EPS) * gain_col


def _layer_kernel(x_ref, mod_ref, pos_ref, gain_ref, invf_ref, wn_ref, wt_ref, wti_ref, wpool_ref,
                  pscale_ref, qg_ref, kg_ref, wa_ref, wb_ref, wo_ref, out_ref,
                  ext_ref, k_ref, ki_ref, vt_ref, qt_ref, qit_ref, widx_ref, key_ref, bias_ref,
                  m_ref, acc_ref, attn_ref, lta_ref, ltb_ref, half_ref, hb_ref, pn_ref, thr_ref,
                  cnt_ref, *, seq_len, top_k):
    i = pl.program_id(1)
    q0 = i * TILE
    nch = (q0 + TILE) // CHUNK
    f32, bf16, i32, i16 = jnp.float32, jnp.bfloat16, jnp.int32, jnp.int16

    x = x_ref[0]
    shift = mod_ref[0, 0:1, :]
    scale = mod_ref[0, 1:2, :]
    gate = mod_ref[0, 2:3, :]
    ms = jnp.mean(x * x, axis=-1, keepdims=True)
    h = (x * lax.rsqrt(ms + EPS) * gain_ref[...]) * (1.0 + scale) + shift
    hb = h.astype(bf16)
    hb_ref[...] = hb

    pt = lax.dot_general(wt_ref[...], hb, _NT_DIMS, preferred_element_type=f32)
    h_mid_f = h - hb.astype(f32)
    h_mid = h_mid_f.astype(bf16)
    h_low = (h_mid_f - h_mid.astype(f32)).astype(bf16)
    p_hi = lax.dot_general(wti_ref[...], hb, _NT_DIMS, preferred_element_type=f32)
    p_mid = lax.dot_general(wti_ref[0:2 * I_END, :], h_mid, _NT_DIMS, preferred_element_type=f32)
    p_low = lax.dot_general(wti_ref[0:I_END, :], h_low, _NT_DIMS, preferred_element_type=f32)
    pti = (((p_mid[I_END:2 * I_END] + p_hi[2 * I_END:3 * I_END]) + p_low)
           + (p_mid[0:I_END] + p_hi[I_END:2 * I_END])) + p_hi[0:I_END]
    ang = invf_ref[...] * pos_ref[0].astype(f32)
    cos = jnp.cos(ang)
    sin = jnp.sin(ang)
    attn_scale = HEAD_DIM ** -0.5 * LOG2_E
    for hd in range(N_HEADS):
        blk = pt[T_Q + hd * HEAD_DIM:T_Q + (hd + 1) * HEAD_DIM]
        blk = _rope_rows(_rms_rows(blk, qg_ref[...]), cos, sin) * attn_scale
        qt_ref[hd * HEAD_DIM:(hd + 1) * HEAD_DIM, :] = blk.astype(bf16)
        q_hi, q_mid, q_low = _split3(_rope_rows(pti[I_Q + hd * IDX_DIM:I_Q + (hd + 1) * IDX_DIM], cos, sin))
        qit_ref[hd * IDX_K:(hd + 1) * IDX_K, :] = jnp.concatenate(
            [q_hi, q_mid, q_hi, q_low, q_hi, q_mid], axis=0).astype(bf16)
    kt = _rope_rows(_rms_rows(pt[T_K:T_V], kg_ref[...]), cos, sin)
    k_hi, k_mid, k_low = _split3(_rope_rows(pti[I_K:I_W], cos, sin))
    kk = jnp.concatenate([k_hi, k_hi, k_mid, k_hi, k_low, k_mid, kt, jnp.zeros_like(kt)], axis=0)
    kk = kk.T.astype(bf16)
    row = lax.broadcasted_iota(i32, (V_ROWS - HEAD_DIM, CHUNK), 0)
    ones_rows = jnp.where(row == 0, 1.0, 0.0).astype(bf16)
    for j in range(TILE // CHUNK):
        c = i * (TILE // CHUNK) + j
        ki_ref[c] = kk[j * CHUNK:(j + 1) * CHUNK, 0:IDX_K]
        k_ref[c] = kk[j * CHUNK:(j + 1) * CHUNK, IDX_K:IDX_K + HEAD_DIM]
        vt_ref[c, 0:HEAD_DIM, :] = pt[T_V:T_END, j * CHUNK:(j + 1) * CHUNK].astype(bf16)
        vt_ref[c, HEAD_DIM:V_ROWS, :] = ones_rows
    idx_w_scale = (IDX_HEADS ** -0.5) * (IDX_DIM ** -0.5)
    widx_ref[...] = pti[I_W:I_W + IDX_HEADS] * idx_w_scale

    key_row = lax.broadcasted_iota(i32, (CHUNK, TILE), 0)
    q_pos = q0 + lax.broadcasted_iota(i32, (CHUNK, TILE), 1)

    def score_chunk(c):
        kic = ki_ref[c]
        score = jnp.zeros((CHUNK, TILE), f32)
        for hd in range(IDX_HEADS):
            s = jnp.dot(kic, qit_ref[hd * IDX_K:(hd + 1) * IDX_K, :], preferred_element_type=f32)
            score = score + jnp.maximum(s, 0.0) * widx_ref[hd:hd + 1, :]
        causal = (c * CHUNK + key_row) <= q_pos
        score = jnp.where(causal, score, -jnp.inf)
        score = jnp.where(score == 0.0, 0.0, score)
        bits = pltpu.bitcast(score, i32)
        key = bits ^ ((bits >> 31) & jnp.int32(0x7FFFFFFF))
        key_ref[c] = key
        half_ref[c] = (key >> HALF_BITS).astype(i16)

    def score_pair(p, carry):
        score_chunk(2 * p)
        score_chunk(2 * p + 1)
        return carry

    lax.fori_loop(0, nch // 2, score_pair, 0)

    @pl.when(nch % 2 == 1)
    def _():
        score_chunk(nch - 1)

    k_f = jnp.full((1, TILE), float(top_k), f32)

    def search(n_chunks):
        def count_ge(mid):
            mid16 = mid.astype(i16)
            acc = None
            for c in range(n_chunks):
                one = jnp.where(half_ref[c] >= mid16, jnp.bfloat16(1), jnp.bfloat16(0))
                parts = [one[s * 32:(s + 1) * 32] for s in range(CHUNK // 32)]
                while len(parts) > 1:
                    parts = [parts[j] + parts[j + 1] for j in range(0, len(parts), 2)]
                acc = parts[0] if acc is None else acc + parts[0]
            return jnp.sum(acc.astype(f32), axis=0, keepdims=True)

        def bisect_half(wanted, cnt_lo0, proj_steps):
            def step(carry):
                lo, hi, cnt_lo, cnt_hi = carry
                mid = (lo + hi) >> 1
                cnt = count_ge(mid)
                ge = cnt >= wanted
                return (jnp.where(ge, mid, lo), jnp.where(ge, hi, mid),
                        jnp.where(ge, cnt, cnt_lo), jnp.where(ge, cnt_hi, cnt))

            def step_with_proj(t, carry):
                for r in range(2):
                    rows = slice(r * TILE // 2, (r + 1) * TILE // 2)
                    pn_ref[t, rows, :] = jnp.dot(hb_ref[rows, :], wn_ref[t], preferred_element_type=f32)
                return step(carry)

            lo0 = jnp.full((1, TILE), HALF_MIN, i32)
            hi0 = jnp.full((1, TILE), HALF_MAX + 1, i32)
            carry = (lo0, hi0, cnt_lo0, jnp.zeros((1, TILE), f32))
            if proj_steps:
                for t in range(HALF_BITS):
                    carry = step_with_proj(t, carry) if t < proj_steps else step(carry)
            else:
                carry = lax.fori_loop(0, HALF_BITS, lambda _, cr: step(cr), carry)
            lo, _, cnt_lo, cnt_hi = carry
            return lo, cnt_lo, cnt_hi

        processed = jnp.full((1, TILE), float(n_chunks * CHUNK), f32)
        thr_hi, cnt_ge_hi, cnt_gt_hi = bisect_half(k_f, processed, N_SLABS)
        for c in range(n_chunks):
            key = key_ref[c]
            low = (key & jnp.int32(HALF_MAX - HALF_MIN)) + HALF_MIN
            in_bucket = (key >> HALF_BITS) == thr_hi
            half_ref[c] = jnp.where(in_bucket, low, HALF_MIN).astype(i16)
        thr_lo, cnt_ge_lo, cnt_gt_lo = bisect_half(k_f - cnt_gt_hi, cnt_ge_hi - cnt_gt_hi, 0)
        thr_ref[...] = (thr_hi << HALF_BITS) + (thr_lo - HALF_MIN)
        cnt_ref[0:1, :] = cnt_gt_hi + cnt_gt_lo
        cnt_ref[1:2, :] = cnt_gt_hi + cnt_ge_lo

    for tile_idx in range(seq_len // TILE):
        pl.when(i == tile_idx)(functools.partial(search, (tile_idx + 1) * TILE // CHUNK))
    thr = thr_ref[...]
    cnt_gt = cnt_ref[0:1, :]
    cnt_ge = cnt_ref[1:2, :]
    ties_wanted = k_f - cnt_gt
    any_ties = jnp.max(cnt_ge - k_f) > 0.0

    def write_bias(c, sel):
        causal = (c * CHUNK + key_row) <= q_pos
        bias_ref[c] = jnp.where(sel & causal, 0.0, -jnp.inf)

    @pl.when(jnp.logical_not(any_ties))
    def _():
        def body(c, carry):
            write_bias(c, key_ref[c] >= thr)
            return carry
        lax.fori_loop(0, nch, body, 0)

    @pl.when(any_ties)
    def _():
        lower_tri = (lax.broadcasted_iota(i32, (CHUNK, CHUNK), 1)
                     <= lax.broadcasted_iota(i32, (CHUNK, CHUNK), 0))
        lower_tri = jnp.where(lower_tri, 1.0, 0.0).astype(bf16)

        def body(c, seen):
            key = key_ref[c]
            tie = key == thr
            tie_f = jnp.where(tie, 1.0, 0.0)
            rank = jnp.dot(lower_tri, tie_f.astype(bf16), preferred_element_type=f32) + seen
            write_bias(c, (key > thr) | (tie & (rank <= ties_wanted)))
            return seen + jnp.sum(tie_f, axis=0, keepdims=True)
        lax.fori_loop(0, nch, body, jnp.zeros((1, TILE), f32))

    m_ref[...] = jnp.full((N_HEADS, TILE), NEG_BIG, f32)
    acc_ref[...] = jnp.zeros((N_HEADS, V_ROWS, TILE), f32)

    def attn_logits(c, buf_ref):
        kc = k_ref[c]
        bias = bias_ref[c]
        col_max = []
        for hd in range(N_HEADS):
            lt = jnp.dot(kc, qt_ref[hd * HEAD_DIM:(hd + 1) * HEAD_DIM, :],
                         preferred_element_type=f32) + bias
            buf_ref[hd] = lt
            col_max.append(jnp.max(lt, axis=0, keepdims=True))
        return tuple(col_max)

    def attn_accumulate(c, buf_ref, col_max):
        vc = vt_ref[c]
        for hd in range(N_HEADS):
            m_old = m_ref[hd:hd + 1, :]
            m_new = jnp.maximum(m_old, col_max[hd])
            p = jnp.exp2(buf_ref[hd] - m_new).astype(bf16)
            alpha = jnp.exp2(m_old - m_new)
            acc_ref[hd] = acc_ref[hd] * alpha + jnp.dot(vc, p, preferred_element_type=f32)
            m_ref[hd:hd + 1, :] = m_new

    def attn_pair(p, max_a):
        c = 2 * p
        max_b = attn_logits(c + 1, ltb_ref)
        attn_accumulate(c, lta_ref, max_a)
        max_a = attn_logits(c + 2, lta_ref)
        attn_accumulate(c + 1, ltb_ref, max_b)
        return max_a

    n_pairs = (nch - 1) // 2
    max_a = lax.fori_loop(0, n_pairs, attn_pair, attn_logits(0, lta_ref))
    c_last = 2 * n_pairs

    @pl.when(c_last == nch - 1)
    def _():
        attn_accumulate(c_last, lta_ref, max_a)

    @pl.when(c_last != nch - 1)
    def _():
        max_b = attn_logits(c_last + 1, ltb_ref)
        attn_accumulate(c_last, lta_ref, max_a)
        attn_accumulate(c_last + 1, ltb_ref, max_b)
    for hd in range(N_HEADS):
        a = acc_ref[hd]
        attn_ref[hd * HEAD_DIM:(hd + 1) * HEAD_DIM, :] = a[0:HEAD_DIM] / a[HEAD_DIM:HEAD_DIM + 1]
    attn = attn_ref[...].T

    def proj_cols(start, stop):
        return jnp.concatenate([pn_ref[s] for s in range(start // SLAB, stop // SLAB)], axis=1)

    u_a = proj_cols(N_UA, N_ZA)

    @pl.when(i == 0)
    def _():
        ext_ref[0:MAX_WINDOW, :] = jnp.zeros((MAX_WINDOW, POOL_WIDTH), f32)

    @pl.when(i > 0)
    def _():
        ext_ref[0:MAX_WINDOW, :] = ext_ref[TILE:TILE + MAX_WINDOW, :]

    ext_ref[MAX_WINDOW:MAX_WINDOW + TILE, :] = u_a
    t_pos = q0 + lax.broadcasted_iota(i32, (TILE, 1), 0)
    mixed = []
    for g, w in enumerate(POOL_WINDOWS):
        cols = slice(g * POOL_GROUP_DIM, (g + 1) * POOL_GROUP_DIM)
        win = ext_ref[MAX_WINDOW:MAX_WINDOW + TILE, cols]
        for j in range(1, w):
            win = win + ext_ref[MAX_WINDOW - j:MAX_WINDOW - j + TILE, cols]
        count = jnp.minimum(t_pos + 1, w).astype(f32)
        pooled = win / count - u_a[:, cols]
        mixed.append(jnp.dot(pooled.astype(bf16), wpool_ref[g], preferred_element_type=f32))
    mixed = jnp.concatenate(mixed, axis=1) * pscale_ref[...]
    z_a = proj_cols(N_ZA, N_ZB)
    y_a = mixed * (z_a * jax.nn.sigmoid(z_a))
    a_out = jnp.dot(y_a.astype(bf16), wa_ref[...], preferred_element_type=f32)

    z_b = proj_cols(N_ZB, N_GA)
    y_b = attn * (z_b * jax.nn.sigmoid(z_b))
    b_out = jnp.dot(y_b.astype(bf16), wb_ref[...], preferred_element_type=f32)
    g_a = proj_cols(N_GA, N_GB)
    g_b = proj_cols(N_GB, N_END)
    merged = jax.nn.sigmoid(g_a) * a_out + jax.nn.sigmoid(g_b) * b_out
    y = jnp.dot(merged.astype(bf16), wo_ref[...], preferred_element_type=f32)
    out_ref[0] = x + gate * y


def _const_spec(shape):
    zeros = (0,) * len(shape)
    return pl.BlockSpec(shape, lambda b, i: zeros, pipeline_mode=pl.Buffered(1))


def _layer(x, mod3, pos3, gain, invf, w_n, w_t, w_ti, w_pool, pscale, qg, kg, w_a, w_b, w_o):
    batch, seq_len, d = x.shape
    assert d == D_MODEL and seq_len % TILE == 0 and TILE % CHUNK == 0
    assert seq_len // 32 <= 256, "bf16 partial counts must stay exactly representable"
    n_chunks = seq_len // CHUNK
    top_k = min(TOPK_MAX, seq_len // 4)
    bf16, f32, i32 = jnp.bfloat16, jnp.float32, jnp.int32
    kernel = functools.partial(_layer_kernel, seq_len=seq_len, top_k=top_k)
    return pl.pallas_call(
        kernel,
        grid=(batch, seq_len // TILE),
        in_specs=[
            pl.BlockSpec((1, TILE, D_MODEL), lambda b, i: (b, i, 0)),
            pl.BlockSpec((1, 3, D_MODEL), lambda b, i: (b, 0, 0)),
            pl.BlockSpec((1, 1, TILE), lambda b, i: (b, 0, i)),
            _const_spec(gain.shape), _const_spec(invf.shape), _const_spec(w_n.shape),
            _const_spec(w_t.shape), _const_spec(w_ti.shape), _const_spec(w_pool.shape), _const_spec(pscale.shape),
            _const_spec(qg.shape), _const_spec(kg.shape), _const_spec(w_a.shape),
            _const_spec(w_b.shape), _const_spec(w_o.shape),
        ],
        out_specs=pl.BlockSpec((1, TILE, D_MODEL), lambda b, i: (b, i, 0)),
        out_shape=jax.ShapeDtypeStruct(x.shape, x.dtype),
        scratch_shapes=[
            pltpu.VMEM((TILE + MAX_WINDOW, POOL_WIDTH), f32),
            pltpu.VMEM((n_chunks, CHUNK, HEAD_DIM), bf16),
            pltpu.VMEM((n_chunks, CHUNK, IDX_K), bf16),
            pltpu.VMEM((n_chunks, V_ROWS, CHUNK), bf16),
            pltpu.VMEM((ATTN_WIDTH, TILE), bf16),
            pltpu.VMEM((IDX_HEADS * IDX_K, TILE), bf16),
            pltpu.VMEM((IDX_HEADS, TILE), f32),
            pltpu.VMEM((n_chunks, CHUNK, TILE), i32),
            pltpu.VMEM((n_chunks, CHUNK, TILE), f32),
            pltpu.VMEM((N_HEADS, TILE), f32),
            pltpu.VMEM((N_HEADS, V_ROWS, TILE), f32),
            pltpu.VMEM((ATTN_WIDTH, TILE), f32),
            pltpu.VMEM((N_HEADS, CHUNK, TILE), f32),
            pltpu.VMEM((N_HEADS, CHUNK, TILE), f32),
            pltpu.VMEM((n_chunks, CHUNK, TILE), jnp.int16),
            pltpu.VMEM((TILE, D_MODEL), bf16),
            pltpu.VMEM((N_SLABS, TILE, SLAB), f32),
            pltpu.VMEM((1, TILE), i32),
            pltpu.VMEM((2, TILE), f32),
        ],
        compiler_params=pltpu.CompilerParams(
            dimension_semantics=("arbitrary", "arbitrary"),
            vmem_limit_bytes=VMEM_LIMIT_BYTES),
        name="hybrid_layer",
    )(x, mod3, pos3, gain, invf, w_n, w_t, w_ti, w_pool, pscale, qg, kg, w_a, w_b, w_o)


def _modulation(c, w_ada, b_ada):
    batch, d = c.shape
    n_out = w_ada.shape[1]
    blk = D_MODEL
    return pl.pallas_call(
        _mod_kernel,
        grid=(n_out // blk,),
        in_specs=[pl.BlockSpec((batch, d), lambda j: (0, 0)),
                  pl.BlockSpec((d, blk), lambda j: (0, j)),
                  pl.BlockSpec((1, blk), lambda j: (0, j))],
        out_specs=pl.BlockSpec((batch, blk), lambda j: (0, j)),
        out_shape=jax.ShapeDtypeStruct((batch, n_out), jnp.float32),
        name="adaln_mod",
    )(c, w_ada, b_ada.reshape(1, n_out))


def kernel(x, c, positions, w_ada, b_ada, norm_gain, w_in, w_pool_group, pool_scale,
           q_norm_gain, k_norm_gain, w_branch_a, w_branch_b, w_out):
    batch, seq_len, _ = x.shape
    bf16 = jnp.bfloat16
    inv_freq = ROPE_THETA ** (-jnp.arange(0, ROT_DIM, 2, dtype=jnp.float32) / ROT_DIM)
    invf = inv_freq.reshape(ROT_HALF, 1)
    pos3 = positions.reshape(batch, 1, seq_len)
    bounds = [0]
    for s in IN_SIZES:
        bounds.append(bounds[-1] + s)
    depth = w_in.shape[0]
    for layer in range(depth):
        seg = [w_in[layer][:, bounds[j]:bounds[j + 1]] for j in range(len(IN_SIZES))]
        u_a, z_a, q, k, v, z_b, q_idx, k_idx, w_idx, g_a, g_b = seg
        w_n = jnp.concatenate([u_a, z_a, z_b, g_a, g_b], axis=1).astype(bf16)
        w_n = w_n.reshape(D_MODEL, N_SLABS, SLAB).transpose(1, 0, 2)
        w_t = jnp.concatenate([q, k, v], axis=1).T.astype(bf16)
        pad = jnp.zeros((D_MODEL, I_END - I_W - IDX_HEADS), jnp.float32)
        w_ti = jnp.concatenate(_split3(jnp.concatenate([q_idx, k_idx, w_idx, pad], axis=1).T),
                               axis=0).astype(bf16)
        mod = _modulation(c, w_ada[layer], b_ada[layer])
        x = _layer(
            x, mod.reshape(batch, 3, D_MODEL), pos3, norm_gain[layer].reshape(1, D_MODEL), invf,
            w_n, w_t, w_ti, w_pool_group[layer].astype(bf16), pool_scale[layer].reshape(1, POOL_WIDTH),
            q_norm_gain[layer].reshape(HEAD_DIM, 1), k_norm_gain[layer].reshape(HEAD_DIM, 1),
            w_branch_a[layer].astype(bf16), w_branch_b[layer].astype(bf16), w_out[layer].astype(bf16))
    return x
```

```python
import functools

import jax
import jax.numpy as jnp
from jax import lax
from jax.experimental import pallas as pl
from jax.experimental.pallas import tpu as pltpu

D_MODEL = 1024
POOL_WIDTH = 512
POOL_GROUPS = 4
POOL_GROUP_DIM = 128
POOL_WINDOWS = (2, 4, 8, 16)
MAX_WINDOW = 16
N_HEADS = 8
HEAD_DIM = 64
ATTN_WIDTH = N_HEADS * HEAD_DIM
IDX_HEADS = 8
IDX_DIM = 64
TOPK_MAX = 256
ROPE_THETA = 500000.0
ROT_DIM = HEAD_DIM // 4
ROT_HALF = ROT_DIM // 2
EPS = 1e-6
IN_SIZES = (POOL_WIDTH, POOL_WIDTH, ATTN_WIDTH, HEAD_DIM, HEAD_DIM, ATTN_WIDTH,
            IDX_HEADS * IDX_DIM, IDX_DIM, IDX_HEADS, D_MODEL, D_MODEL)

TILE = 256
CHUNK = 256
V_ROWS = 80
NEG_BIG = -1e30
LOG2_E = 1.4426950408889634
HALF_BITS = 16
HALF_MIN, HALF_MAX = -(1 << 15), (1 << 15) - 1
V7X_VMEM_BYTES = 64 * 1024 * 1024
VMEM_LIMIT_BYTES = V7X_VMEM_BYTES * 7 // 8

T_Q, T_K, T_V, T_END = 0, 512, 576, 640
I_Q, I_K, I_W, I_END = 0, 512, 576, 592
IDX_K = 6 * IDX_DIM
N_UA, N_ZA, N_ZB, N_GA, N_GB, N_END = 0, 512, 1024, 1536, 2560, 3584
SLAB = 256
N_SLABS = N_END // SLAB
assert N_SLABS <= HALF_BITS

_NT_DIMS = (((1,), (1,)), ((), ()))


def _mod_kernel(c_ref, w_ref, b_ref, o_ref):
    c = c_ref[...]
    s = c * jax.nn.sigmoid(c)
    o_ref[...] = jnp.dot(s, w_ref[...], preferred_element_type=jnp.float32,
                         precision=lax.Precision.HIGHEST) + b_ref[...]


def _rope_rows(blk, cos, sin):
    x1 = blk[0:ROT_HALF]
    x2 = blk[ROT_HALF:ROT_DIM]
    return jnp.concatenate([x1 * cos - x2 * sin, x2 * cos + x1 * sin, blk[ROT_DIM:]], axis=0)


def _split3(x):
    hi = x.astype(jnp.bfloat16).astype(jnp.float32)
    rest = x - hi
    mid = rest.astype(jnp.bfloat16).astype(jnp.float32)
    low = (rest - mid).astype(jnp.bfloat16).astype(jnp.float32)
    return hi, mid, low


def _rms_rows(blk, gain_col):
    ms = jnp.mean(blk * blk, axis=0, keepdims=True)
    return blk * lax.rsqrt(ms + EPS) * gain_col


def _layer_kernel(x_ref, mod_ref, pos_ref, gain_ref, invf_ref, wn_ref, wt_ref, wti_ref, wpool_ref,
                  pscale_ref, qg_ref, kg_ref, wa_ref, wb_ref, wo_ref, out_ref,
                  ext_ref, k_ref, ki_ref, vt_ref, qt_ref, qit_ref, widx_ref, key_ref, bias_ref,
                  m_ref, acc_ref, attn_ref, lta_ref, ltb_ref, half_ref, hb_ref, pn_ref, thr_ref,
                  cnt_ref, *, seq_len, top_k):
    i = pl.program_id(1)
    q0 = i * TILE
    nch = (q0 + TILE) // CHUNK
    f32, bf16, i32, i16 = jnp.float32, jnp.bfloat16, jnp.int32, jnp.int16

    x = x_ref[0]
    shift = mod_ref[0, 0:1, :]
    scale = mod_ref[0, 1:2, :]
    gate = mod_ref[0, 2:3, :]
    ms = jnp.mean(x * x, axis=-1, keepdims=True)
    h = (x * lax.rsqrt(ms + EPS) * gain_ref[...]) * (1.0 + scale) + shift
    hb = h.astype(bf16)
    hb_ref[...] = hb

    pt = lax.dot_general(wt_ref[...], hb, _NT_DIMS, preferred_element_type=f32)
    h_mid_f = h - hb.astype(f32)
    h_mid = h_mid_f.astype(bf16)
    h_low = (h_mid_f - h_mid.astype(f32)).astype(bf16)
    p_hi = lax.dot_general(wti_ref[...], hb, _NT_DIMS, preferred_element_type=f32)
    p_mid = lax.dot_general(wti_ref[0:2 * I_END, :], h_mid, _NT_DIMS, preferred_element_type=f32)
    p_low = lax.dot_general(wti_ref[0:I_END, :], h_low, _NT_DIMS, preferred_element_type=f32)
    pti = (((p_mid[I_END:2 * I_END] + p_hi[2 * I_END:3 * I_END]) + p_low)
           + (p_mid[0:I_END] + p_hi[I_END:2 * I_END])) + p_hi[0:I_END]
    ang = invf_ref[...] * pos_ref[0].astype(f32)
    cos = jnp.cos(ang)
    sin = jnp.sin(ang)
    attn_scale = HEAD_DIM ** -0.5 * LOG2_E
    for hd in range(N_HEADS):
        blk = pt[T_Q + hd * HEAD_DIM:T_Q + (hd + 1) * HEAD_DIM]
        blk = _rope_rows(_rms_rows(blk, qg_ref[...]), cos, sin) * attn_scale
        qt_ref[hd * HEAD_DIM:(hd + 1) * HEAD_DIM, :] = blk.astype(bf16)
        q_hi, q_mid, q_low = _split3(_rope_rows(pti[I_Q + hd * IDX_DIM:I_Q + (hd + 1) * IDX_DIM], cos, sin))
        qit_ref[hd * IDX_K:(hd + 1) * IDX_K, :] = jnp.concatenate(
            [q_hi, q_mid, q_hi, q_low, q_hi, q_mid], axis=0).astype(bf16)
    kt = _rope_rows(_rms_rows(pt[T_K:T_V], kg_ref[...]), cos, sin)
    k_hi, k_mid, k_low = _split3(_rope_rows(pti[I_K:I_W], cos, sin))
    kk = jnp.concatenate([k_hi, k_hi, k_mid, k_hi, k_low, k_mid, kt, jnp.zeros_like(kt)], axis=0)
    kk = kk.T.astype(bf16)
    row = lax.broadcasted_iota(i32, (V_ROWS - HEAD_DIM, CHUNK), 0)
    ones_rows = jnp.where(row == 0, 1.0, 0.0).astype(bf16)
    for j in range(TILE // CHUNK):
        c = i * (TILE // CHUNK) + j
        ki_ref[c] = kk[j * CHUNK:(j + 1) * CHUNK, 0:IDX_K]
        k_ref[c] = kk[j * CHUNK:(j + 1) * CHUNK, IDX_K:IDX_K + HEAD_DIM]
        vt_ref[c, 0:HEAD_DIM, :] = pt[T_V:T_END, j * CHUNK:(j + 1) * CHUNK].astype(bf16)
        vt_ref[c, HEAD_DIM:V_ROWS, :] = ones_rows
    idx_w_scale = (IDX_HEADS ** -0.5) * (IDX_DIM ** -0.5)
    widx_ref[...] = pti[I_W:I_W + IDX_HEADS] * idx_w_scale

    key_row = lax.broadcasted_iota(i32, (CHUNK, TILE), 0)
    q_pos = q0 + lax.broadcasted_iota(i32, (CHUNK, TILE), 1)

    def score_chunk(c):
        kic = ki_ref[c]
        score = jnp.zeros((CHUNK, TILE), f32)
        for hd in range(IDX_HEADS):
            s = jnp.dot(kic, qit_ref[hd * IDX_K:(hd + 1) * IDX_K, :], preferred_element_type=f32)
            score = score + jnp.maximum(s, 0.0) * widx_ref[hd:hd + 1, :]
        causal = (c * CHUNK + key_row) <= q_pos
        score = jnp.where(causal, score, -jnp.inf)
        score = jnp.where(score == 0.0, 0.0, score)
        bits = pltpu.bitcast(score, i32)
        key = bits ^ ((bits >> 31) & jnp.int32(0x7FFFFFFF))
        key_ref[c] = key
        half_ref[c] = (key >> HALF_BITS).astype(i16)

    def score_chunks(first, count):
        for j in range(count):
            score_chunk(first + j)

    def score_quad(q, carry):
        score_chunks(4 * q, 4)
        return carry

    lax.fori_loop(0, nch // 4, score_quad, 0)

    @pl.when(nch % 4 >= 2)
    def _():
        score_chunks(nch - nch % 4, 2)

    @pl.when(nch % 2 == 1)
    def _():
        score_chunk(nch - 1)

    k_f = jnp.full((1, TILE), float(top_k), f32)

    def search(n_chunks):
        def count_ge(mid):
            mid16 = mid.astype(i16)
            acc = None
            for c in range(n_chunks):
                one = jnp.where(half_ref[c] >= mid16, jnp.bfloat16(1), jnp.bfloat16(0))
                parts = [one[s * 32:(s + 1) * 32] for s in range(CHUNK // 32)]
                while len(parts) > 1:
                    parts = [parts[j] + parts[j + 1] for j in range(0, len(parts), 2)]
                acc = parts[0] if acc is None else acc + parts[0]
            return jnp.sum(acc.astype(f32), axis=0, keepdims=True)

        def bisect_half(wanted, cnt_lo0, proj_steps):
            def step(carry):
                lo, hi, cnt_lo, cnt_hi = carry
                mid = (lo + hi) >> 1
                cnt = count_ge(mid)
                ge = cnt >= wanted
                return (jnp.where(ge, mid, lo), jnp.where(ge, hi, mid),
                        jnp.where(ge, cnt, cnt_lo), jnp.where(ge, cnt_hi, cnt))

            def step_with_proj(t, carry):
                for r in range(2):
                    rows = slice(r * TILE // 2, (r + 1) * TILE // 2)
                    pn_ref[t, rows, :] = jnp.dot(hb_ref[rows, :], wn_ref[t], preferred_element_type=f32)
                return step(carry)

            lo0 = jnp.full((1, TILE), HALF_MIN, i32)
            hi0 = jnp.full((1, TILE), HALF_MAX + 1, i32)
            carry = (lo0, hi0, cnt_lo0, jnp.zeros((1, TILE), f32))
            if proj_steps:
                for t in range(HALF_BITS):
                    carry = step_with_proj(t, carry) if t < proj_steps else step(carry)
            else:
                carry = lax.fori_loop(0, HALF_BITS, lambda _, cr: step(cr), carry)
            lo, _, cnt_lo, cnt_hi = carry
            return lo, cnt_lo, cnt_hi

        processed = jnp.full((1, TILE), float(n_chunks * CHUNK), f32)
        thr_hi, cnt_ge_hi, cnt_gt_hi = bisect_half(k_f, processed, N_SLABS)
        for c in range(n_chunks):
            key = key_ref[c]
            low = (key & jnp.int32(HALF_MAX - HALF_MIN)) + HALF_MIN
            in_bucket = (key >> HALF_BITS) == thr_hi
            half_ref[c] = jnp.where(in_bucket, low, HALF_MIN).astype(i16)
        thr_lo, cnt_ge_lo, cnt_gt_lo = bisect_half(k_f - cnt_gt_hi, cnt_ge_hi - cnt_gt_hi, 0)
        thr_ref[...] = (thr_hi << HALF_BITS) + (thr_lo - HALF_MIN)
        cnt_ref[0:1, :] = cnt_gt_hi + cnt_gt_lo
        cnt_ref[1:2, :] = cnt_gt_hi + cnt_ge_lo

    for tile_idx in range(seq_len // TILE):
        pl.when(i == tile_idx)(functools.partial(search, (tile_idx + 1) * TILE // CHUNK))
    thr = thr_ref[...]
    cnt_gt = cnt_ref[0:1, :]
    cnt_ge = cnt_ref[1:2, :]
    ties_wanted = k_f - cnt_gt
    any_ties = jnp.max(cnt_ge - k_f) > 0.0

    def write_bias(c, sel):
        causal = (c * CHUNK + key_row) <= q_pos
        bias_ref[c] = jnp.where(sel & causal, 0.0, -jnp.inf)

    @pl.when(jnp.logical_not(any_ties))
    def _():
        n_diag = TILE // CHUNK

        def body(c, carry):
            bias_ref[c] = jnp.where(key_ref[c] >= thr, 0.0, -jnp.inf)
            return carry
        lax.fori_loop(0, nch - n_diag, body, 0)
        for j in range(n_diag):
            c = nch - n_diag + j
            write_bias(c, key_ref[c] >= thr)

    @pl.when(any_ties)
    def _():
        lower_tri = (lax.broadcasted_iota(i32, (CHUNK, CHUNK), 1)
                     <= lax.broadcasted_iota(i32, (CHUNK, CHUNK), 0))
        lower_tri = jnp.where(lower_tri, 1.0, 0.0).astype(bf16)

        def body(c, seen):
            key = key_ref[c]
            tie = key == thr
            tie_f = jnp.where(tie, 1.0, 0.0)
            rank = jnp.dot(lower_tri, tie_f.astype(bf16), preferred_element_type=f32) + seen
            write_bias(c, (key > thr) | (tie & (rank <= ties_wanted)))
            return seen + jnp.sum(tie_f, axis=0, keepdims=True)
        lax.fori_loop(0, nch, body, jnp.zeros((1, TILE), f32))

    m_ref[...] = jnp.full((N_HEADS, TILE), NEG_BIG, f32)
    acc_ref[...] = jnp.zeros((N_HEADS, V_ROWS, TILE), f32)

    def attn_logits(c, buf_ref):
        kc = k_ref[c]
        bias = bias_ref[c]
        col_max = []
        for hd in range(N_HEADS):
            lt = jnp.dot(kc, qt_ref[hd * HEAD_DIM:(hd + 1) * HEAD_DIM, :],
                         preferred_element_type=f32) + bias
            buf_ref[hd] = lt
            col_max.append(jnp.max(lt, axis=0, keepdims=True))
        return tuple(col_max)

    def attn_accumulate(c, buf_ref, col_max):
        vc = vt_ref[c]
        for hd in range(N_HEADS):
            m_old = m_ref[hd:hd + 1, :]
            m_new = jnp.maximum(m_old, col_max[hd])
            p = jnp.exp2(buf_ref[hd] - m_new).astype(bf16)
            alpha = jnp.exp2(m_old - m_new)
            acc_ref[hd] = acc_ref[hd] * alpha + jnp.dot(vc, p, preferred_element_type=f32)
            m_ref[hd:hd + 1, :] = m_new

    def attn_pair(p, max_a):
        c = 2 * p
        max_b = attn_logits(c + 1, ltb_ref)
        attn_accumulate(c, lta_ref, max_a)
        max_a = attn_logits(c + 2, lta_ref)
        attn_accumulate(c + 1, ltb_ref, max_b)
        return max_a

    n_pairs = (nch - 1) // 2
    max_a = lax.fori_loop(0, n_pairs, attn_pair, attn_logits(0, lta_ref))
    c_last = 2 * n_pairs

    @pl.when(c_last == nch - 1)
    def _():
        attn_accumulate(c_last, lta_ref, max_a)

    @pl.when(c_last != nch - 1)
    def _():
        max_b = attn_logits(c_last + 1, ltb_ref)
        attn_accumulate(c_last, lta_ref, max_a)
        attn_accumulate(c_last + 1, ltb_ref, max_b)
    for hd in range(N_HEADS):
        a = acc_ref[hd]
        attn_ref[hd * HEAD_DIM:(hd + 1) * HEAD_DIM, :] = a[0:HEAD_DIM] / a[HEAD_DIM:HEAD_DIM + 1]
    attn = attn_ref[...].T

    def proj_cols(start, stop):
        return jnp.concatenate([pn_ref[s] for s in range(start // SLAB, stop // SLAB)], axis=1)

    u_a = proj_cols(N_UA, N_ZA)

    @pl.when(i == 0)
    def _():
        ext_ref[0:MAX_WINDOW, :] = jnp.zeros((MAX_WINDOW, POOL_WIDTH), f32)

    @pl.when(i > 0)
    def _():
        ext_ref[0:MAX_WINDOW, :] = ext_ref[TILE:TILE + MAX_WINDOW, :]

    ext_ref[MAX_WINDOW:MAX_WINDOW + TILE, :] = u_a
    t_pos = q0 + lax.broadcasted_iota(i32, (TILE, 1), 0)
    mixed = []
    for g, w in enumerate(POOL_WINDOWS):
        cols = slice(g * POOL_GROUP_DIM, (g + 1) * POOL_GROUP_DIM)
        win = ext_ref[MAX_WINDOW:MAX_WINDOW + TILE, cols]
        for j in range(1, w):
            win = win + ext_ref[MAX_WINDOW - j:MAX_WINDOW - j + TILE, cols]
        count = jnp.minimum(t_pos + 1, w).astype(f32)
        pooled = win / count - u_a[:, cols]
        mixed.append(jnp.dot(pooled.astype(bf16), wpool_ref[g], preferred_element_type=f32))
    mixed = jnp.concatenate(mixed, axis=1) * pscale_ref[...]
    z_a = proj_cols(N_ZA, N_ZB)
    y_a = mixed * (z_a * jax.nn.sigmoid(z_a))
    a_out = jnp.dot(y_a.astype(bf16), wa_ref[...], preferred_element_type=f32)

    z_b = proj_cols(N_ZB, N_GA)
    y_b = attn * (z_b * jax.nn.sigmoid(z_b))
    b_out = jnp.dot(y_b.astype(bf16), wb_ref[...], preferred_element_type=f32)
    g_a = proj_cols(N_GA, N_GB)
    g_b = proj_cols(N_GB, N_END)
    merged = jax.nn.sigmoid(g_a) * a_out + jax.nn.sigmoid(g_b) * b_out
    y = jnp.dot(merged.astype(bf16), wo_ref[...], preferred_element_type=f32)
    out_ref[0] = x + gate * y


def _const_spec(shape):
    zeros = (0,) * len(shape)
    return pl.BlockSpec(shape, lambda b, i: zeros, pipeline_mode=pl.Buffered(1))


def _layer(x, mod3, pos3, gain, invf, w_n, w_t, w_ti, w_pool, pscale, qg, kg, w_a, w_b, w_o):
    batch, seq_len, d = x.shape
    assert d == D_MODEL and seq_len % TILE == 0 and TILE % CHUNK == 0
    assert seq_len // 32 <= 256, "bf16 partial counts must stay exactly representable"
    n_chunks = seq_len // CHUNK
    top_k = min(TOPK_MAX, seq_len // 4)
    bf16, f32, i32 = jnp.bfloat16, jnp.float32, jnp.int32
    kernel = functools.partial(_layer_kernel, seq_len=seq_len, top_k=top_k)
    return pl.pallas_call(
        kernel,
        grid=(batch, seq_len // TILE),
        in_specs=[
            pl.BlockSpec((1, TILE, D_MODEL), lambda b, i: (b, i, 0)),
            pl.BlockSpec((1, 3, D_MODEL), lambda b, i: (b, 0, 0)),
            pl.BlockSpec((1, 1, TILE), lambda b, i: (b, 0, i)),
            _const_spec(gain.shape), _const_spec(invf.shape), _const_spec(w_n.shape),
            _const_spec(w_t.shape), _const_spec(w_ti.shape), _const_spec(w_pool.shape), _const_spec(pscale.shape),
            _const_spec(qg.shape), _const_spec(kg.shape), _const_spec(w_a.shape),
            _const_spec(w_b.shape), _const_spec(w_o.shape),
        ],
        out_specs=pl.BlockSpec((1, TILE, D_MODEL), lambda b, i: (b, i, 0)),
        out_shape=jax.ShapeDtypeStruct(x.shape, x.dtype),
        scratch_shapes=[
            pltpu.VMEM((TILE + MAX_WINDOW, POOL_WIDTH), f32),
            pltpu.VMEM((n_chunks, CHUNK, HEAD_DIM), bf16),
            pltpu.VMEM((n_chunks, CHUNK, IDX_K), bf16),
            pltpu.VMEM((n_chunks, V_ROWS, CHUNK), bf16),
            pltpu.VMEM((ATTN_WIDTH, TILE), bf16),
            pltpu.VMEM((IDX_HEADS * IDX_K, TILE), bf16),
            pltpu.VMEM((IDX_HEADS, TILE), f32),
            pltpu.VMEM((n_chunks, CHUNK, TILE), i32),
            pltpu.VMEM((n_chunks, CHUNK, TILE), f32),
            pltpu.VMEM((N_HEADS, TILE), f32),
            pltpu.VMEM((N_HEADS, V_ROWS, TILE), f32),
            pltpu.VMEM((ATTN_WIDTH, TILE), f32),
            pltpu.VMEM((N_HEADS, CHUNK, TILE), f32),
            pltpu.VMEM((N_HEADS, CHUNK, TILE), f32),
            pltpu.VMEM((n_chunks, CHUNK, TILE), jnp.int16),
            pltpu.VMEM((TILE, D_MODEL), bf16),
            pltpu.VMEM((N_SLABS, TILE, SLAB), f32),
            pltpu.VMEM((1, TILE), i32),
            pltpu.VMEM((2, TILE), f32),
        ],
        compiler_params=pltpu.CompilerParams(
            dimension_semantics=("arbitrary", "arbitrary"),
            vmem_limit_bytes=VMEM_LIMIT_BYTES),
        name="hybrid_layer",
    )(x, mod3, pos3, gain, invf, w_n, w_t, w_ti, w_pool, pscale, qg, kg, w_a, w_b, w_o)


def _modulation(c, w_ada, b_ada):
    batch, d = c.shape
    n_out = w_ada.shape[1]
    blk = D_MODEL
    return pl.pallas_call(
        _mod_kernel,
        grid=(n_out // blk,),
        in_specs=[pl.BlockSpec((batch, d), lambda j: (0, 0)),
                  pl.BlockSpec((d, blk), lambda j: (0, j)),
                  pl.BlockSpec((1, blk), lambda j: (0, j))],
        out_specs=pl.BlockSpec((batch, blk), lambda j: (0, j)),
        out_shape=jax.ShapeDtypeStruct((batch, n_out), jnp.float32),
        name="adaln_mod",
    )(c, w_ada, b_ada.reshape(1, n_out))


def kernel(x, c, positions, w_ada, b_ada, norm_gain, w_in, w_pool_group, pool_scale,
           q_norm_gain, k_norm_gain, w_branch_a, w_branch_b, w_out):
    batch, seq_len, _ = x.shape
    bf16 = jnp.bfloat16
    inv_freq = ROPE_THETA ** (-jnp.arange(0, ROT_DIM, 2, dtype=jnp.float32) / ROT_DIM)
    invf = inv_freq.reshape(ROT_HALF, 1)
    pos3 = positions.reshape(batch, 1, seq_len)
    bounds = [0]
    for s in IN_SIZES:
        bounds.append(bounds[-1] + s)
    depth = w_in.shape[0]
    for layer in range(depth):
        seg = [w_in[layer][:, bounds[j]:bounds[j + 1]] for j in range(len(IN_SIZES))]
        u_a, z_a, q, k, v, z_b, q_idx, k_idx, w_idx, g_a, g_b = seg
        w_n = jnp.concatenate([u_a, z_a, z_b, g_a, g_b], axis=1).astype(bf16)
        w_n = w_n.reshape(D_MODEL, N_SLABS, SLAB).transpose(1, 0, 2)
        w_t = jnp.concatenate([q, k, v], axis=1).T.astype(bf16)
        pad = jnp.zeros((D_MODEL, I_END - I_W - IDX_HEADS), jnp.float32)
        w_ti = jnp.concatenate(_split3(jnp.concatenate([q_idx, k_idx, w_idx, pad], axis=1).T),
                               axis=0).astype(bf16)
        mod = _modulation(c, w_ada[layer], b_ada[layer])
        x = _layer(
            x, mod.reshape(batch, 3, D_MODEL), pos3, norm_gain[layer].reshape(1, D_MODEL), invf,
            w_n, w_t, w_ti, w_pool_group[layer].astype(bf16), pool_scale[layer].reshape(1, POOL_WIDTH),
            q_norm_gain[layer].reshape(HEAD_DIM, 1), k_norm_gain[layer].reshape(HEAD_DIM, 1),
            w_branch_a[layer].astype(bf16), w_branch_b[layer].astype(bf16), w_out[layer].astype(bf16))
    return x
```

```python
import functools

import jax
import jax.numpy as jnp
from jax import lax
from jax.experimental import pallas as pl
from jax.experimental.pallas import tpu as pltpu

D_MODEL = 1024
POOL_WIDTH = 512
POOL_GROUPS = 4
POOL_GROUP_DIM = 128
POOL_WINDOWS = (2, 4, 8, 16)
MAX_WINDOW = 16
N_HEADS = 8
HEAD_DIM = 64
ATTN_WIDTH = N_HEADS * HEAD_DIM
IDX_HEADS = 8
IDX_DIM = 64
TOPK_MAX = 256
ROPE_THETA = 500000.0
ROT_DIM = HEAD_DIM // 4
ROT_HALF = ROT_DIM // 2
EPS = 1e-6
IN_SIZES = (POOL_WIDTH, POOL_WIDTH, ATTN_WIDTH, HEAD_DIM, HEAD_DIM, ATTN_WIDTH,
            IDX_HEADS * IDX_DIM, IDX_DIM, IDX_HEADS, D_MODEL, D_MODEL)

TILE = 256
CHUNK = 256
V_ROWS = 80
NEG_BIG = -1e30
LOG2_E = 1.4426950408889634
HALF_BITS = 16
HALF_MIN, HALF_MAX = -(1 << 15), (1 << 15) - 1
V7X_VMEM_BYTES = 64 * 1024 * 1024
VMEM_LIMIT_BYTES = V7X_VMEM_BYTES * 7 // 8

T_Q, T_K, T_V, T_END = 0, 512, 576, 640
I_Q, I_K, I_W, I_END = 0, 512, 576, 592
IDX_K = 6 * IDX_DIM
N_UA, N_ZA, N_ZB, N_GA, N_GB, N_END = 0, 512, 1024, 1536, 2560, 3584
SLAB = 256
N_SLABS = N_END // SLAB
assert N_SLABS <= HALF_BITS

_NT_DIMS = (((1,), (1,)), ((), ()))


def _mod_kernel(c_ref, w_ref, b_ref, o_ref):
    c = c_ref[...]
    s = c * jax.nn.sigmoid(c)
    o_ref[...] = jnp.dot(s, w_ref[...], preferred_element_type=jnp.float32,
                         precision=lax.Precision.HIGHEST) + b_ref[...]


def _rope_rows(blk, cos, sin):
    x1 = blk[0:ROT_HALF]
    x2 = blk[ROT_HALF:ROT_DIM]
    return jnp.concatenate([x1 * cos - x2 * sin, x2 * cos + x1 * sin, blk[ROT_DIM:]], axis=0)


def _split3(x):
    hi = x.astype(jnp.bfloat16).astype(jnp.float32)
    rest = x - hi
    mid = rest.astype(jnp.bfloat16).astype(jnp.float32)
    low = (rest - mid).astype(jnp.bfloat16).astype(jnp.float32)
    return hi, mid, low


def _rms_rows(blk, gain_col):
    ms = jnp.mean(blk * blk, axis=0, keepdims=True)
    return blk * lax.rsqrt(ms + EPS) * gain_col


def _layer_kernel(x_ref, mod_ref, pos_ref, gain_ref, invf_ref, wn_ref, wt_ref, wti_ref, wpool_ref,
                  pscale_ref, qg_ref, kg_ref, wa_ref, wb_ref, wo_ref, out_ref,
                  ext_ref, k_ref, ki_ref, vt_ref, qt_ref, qit_ref, widx_ref, key_ref, bias_ref,
                  m_ref, acc_ref, attn_ref, lta_ref, ltb_ref, half_ref, hb_ref, pn_ref, thr_ref,
                  cnt_ref, *, seq_len, top_k):
    i = pl.program_id(1)
    q0 = i * TILE
    nch = (q0 + TILE) // CHUNK
    f32, bf16, i32, i16 = jnp.float32, jnp.bfloat16, jnp.int32, jnp.int16

    x = x_ref[0]
    shift = mod_ref[0, 0:1, :]
    scale = mod_ref[0, 1:2, :]
    gate = mod_ref[0, 2:3, :]
    ms = jnp.mean(x * x, axis=-1, keepdims=True)
    h = (x * lax.rsqrt(ms + EPS) * gain_ref[...]) * (1.0 + scale) + shift
    hb = h.astype(bf16)
    hb_ref[...] = hb

    pt = lax.dot_general(wt_ref[...], hb, _NT_DIMS, preferred_element_type=f32)
    h_mid_f = h - hb.astype(f32)
    h_mid = h_mid_f.astype(bf16)
    h_low = (h_mid_f - h_mid.astype(f32)).astype(bf16)
    p_hi = lax.dot_general(wti_ref[...], hb, _NT_DIMS, preferred_element_type=f32)
    p_mid = lax.dot_general(wti_ref[0:2 * I_END, :], h_mid, _NT_DIMS, preferred_element_type=f32)
    p_low = lax.dot_general(wti_ref[0:I_END, :], h_low, _NT_DIMS, preferred_element_type=f32)
    pti = (((p_mid[I_END:2 * I_END] + p_hi[2 * I_END:3 * I_END]) + p_low)
           + (p_mid[0:I_END] + p_hi[I_END:2 * I_END])) + p_hi[0:I_END]
    ang = invf_ref[...] * pos_ref[0].astype(f32)
    cos = jnp.cos(ang)
    sin = jnp.sin(ang)
    attn_scale = HEAD_DIM ** -0.5 * LOG2_E
    for hd in range(N_HEADS):
        blk = pt[T_Q + hd * HEAD_DIM:T_Q + (hd + 1) * HEAD_DIM]
        blk = _rope_rows(_rms_rows(blk, qg_ref[...]), cos, sin) * attn_scale
        qt_ref[hd * HEAD_DIM:(hd + 1) * HEAD_DIM, :] = blk.astype(bf16)
        q_hi, q_mid, q_low = _split3(_rope_rows(pti[I_Q + hd * IDX_DIM:I_Q + (hd + 1) * IDX_DIM], cos, sin))
        qit_ref[hd * IDX_K:(hd + 1) * IDX_K, :] = jnp.concatenate(
            [q_hi, q_mid, q_hi, q_low, q_hi, q_mid], axis=0).astype(bf16)
    kt = _rope_rows(_rms_rows(pt[T_K:T_V], kg_ref[...]), cos, sin)
    k_hi, k_mid, k_low = _split3(_rope_rows(pti[I_K:I_W], cos, sin))
    kk = jnp.concatenate([k_hi, k_hi, k_mid, k_hi, k_low, k_mid, kt, jnp.zeros_like(kt)], axis=0)
    kk = kk.T.astype(bf16)
    row = lax.broadcasted_iota(i32, (V_ROWS - HEAD_DIM, CHUNK), 0)
    ones_rows = jnp.where(row == 0, 1.0, 0.0).astype(bf16)
    for j in range(TILE // CHUNK):
        c = i * (TILE // CHUNK) + j
        ki_ref[c] = kk[j * CHUNK:(j + 1) * CHUNK, 0:IDX_K]
        k_ref[c] = kk[j * CHUNK:(j + 1) * CHUNK, IDX_K:IDX_K + HEAD_DIM]
        vt_ref[c, 0:HEAD_DIM, :] = pt[T_V:T_END, j * CHUNK:(j + 1) * CHUNK].astype(bf16)
        vt_ref[c, HEAD_DIM:V_ROWS, :] = ones_rows
    idx_w_scale = (IDX_HEADS ** -0.5) * (IDX_DIM ** -0.5)
    widx_ref[...] = pti[I_W:I_W + IDX_HEADS] * idx_w_scale

    key_row = lax.broadcasted_iota(i32, (CHUNK, TILE), 0)
    q_pos = q0 + lax.broadcasted_iota(i32, (CHUNK, TILE), 1)

    def score_chunk(c):
        kic = ki_ref[c]
        score = jnp.zeros((CHUNK, TILE), f32)
        for hd in range(IDX_HEADS):
            s = jnp.dot(kic, qit_ref[hd * IDX_K:(hd + 1) * IDX_K, :], preferred_element_type=f32)
            score = score + jnp.maximum(s, 0.0) * widx_ref[hd:hd + 1, :]
        causal = (c * CHUNK + key_row) <= q_pos
        score = jnp.where(causal, score, -jnp.inf)
        score = jnp.where(score == 0.0, 0.0, score)
        bits = pltpu.bitcast(score, i32)
        key = bits ^ ((bits >> 31) & jnp.int32(0x7FFFFFFF))
        key_ref[c] = key
        half_ref[c] = (key >> HALF_BITS).astype(i16)

    def score_chunks(first, count):
        for j in range(count):
            score_chunk(first + j)

    def score_quad(q, carry):
        score_chunks(4 * q, 4)
        return carry

    lax.fori_loop(0, nch // 4, score_quad, 0)

    @pl.when(nch % 4 >= 2)
    def _():
        score_chunks(nch - nch % 4, 2)

    @pl.when(nch % 2 == 1)
    def _():
        score_chunk(nch - 1)

    k_f = jnp.full((1, TILE), float(top_k), f32)

    def search(n_chunks):
        def count_ge(mid):
            mid16 = mid.astype(i16)
            acc = None
            for c in range(n_chunks):
                one = jnp.where(half_ref[c] >= mid16, jnp.bfloat16(1), jnp.bfloat16(0))
                parts = [one[s * 32:(s + 1) * 32] for s in range(CHUNK // 32)]
                while len(parts) > 1:
                    parts = [parts[j] + parts[j + 1] for j in range(0, len(parts), 2)]
                acc = parts[0] if acc is None else acc + parts[0]
            return jnp.sum(acc.astype(f32), axis=0, keepdims=True)

        def bisect_half(wanted, cnt_lo0, proj_steps):
            def step(carry):
                lo, hi, cnt_lo, cnt_hi = carry
                mid = (lo + hi) >> 1
                cnt = count_ge(mid)
                ge = cnt >= wanted
                return (jnp.where(ge, mid, lo), jnp.where(ge, hi, mid),
                        jnp.where(ge, cnt, cnt_lo), jnp.where(ge, cnt_hi, cnt))

            def step_with_proj(t, carry):
                for r in range(2):
                    rows = slice(r * TILE // 2, (r + 1) * TILE // 2)
                    pn_ref[t, rows, :] = jnp.dot(hb_ref[rows, :], wn_ref[t], preferred_element_type=f32)
                return step(carry)

            lo0 = jnp.full((1, TILE), HALF_MIN, i32)
            hi0 = jnp.full((1, TILE), HALF_MAX + 1, i32)
            carry = (lo0, hi0, cnt_lo0, jnp.zeros((1, TILE), f32))
            for t in range(HALF_BITS):
                slab = proj_steps[t // 2] if t % 2 == 0 and t // 2 < len(proj_steps) else None
                carry = step(carry) if slab is None else step_with_proj(slab, carry)
            lo, _, cnt_lo, cnt_hi = carry
            return lo, cnt_lo, cnt_hi

        processed = jnp.full((1, TILE), float(n_chunks * CHUNK), f32)
        slabs = tuple(range(N_SLABS))
        thr_hi, cnt_ge_hi, cnt_gt_hi = bisect_half(k_f, processed, slabs[:N_SLABS // 2])
        for c in range(n_chunks):
            key = key_ref[c]
            low = (key & jnp.int32(HALF_MAX - HALF_MIN)) + HALF_MIN
            in_bucket = (key >> HALF_BITS) == thr_hi
            half_ref[c] = jnp.where(in_bucket, low, HALF_MIN).astype(i16)
        thr_lo, cnt_ge_lo, cnt_gt_lo = bisect_half(k_f - cnt_gt_hi, cnt_ge_hi - cnt_gt_hi, slabs[N_SLABS // 2:])
        thr_ref[...] = (thr_hi << HALF_BITS) + (thr_lo - HALF_MIN)
        cnt_ref[0:1, :] = cnt_gt_hi + cnt_gt_lo
        cnt_ref[1:2, :] = cnt_gt_hi + cnt_ge_lo

    for tile_idx in range(seq_len // TILE):
        pl.when(i == tile_idx)(functools.partial(search, (tile_idx + 1) * TILE // CHUNK))
    thr = thr_ref[...]
    cnt_gt = cnt_ref[0:1, :]
    cnt_ge = cnt_ref[1:2, :]
    ties_wanted = k_f - cnt_gt
    any_ties = jnp.max(cnt_ge - k_f) > 0.0

    def write_bias(c, sel):
        causal = (c * CHUNK + key_row) <= q_pos
        bias_ref[c] = jnp.where(sel & causal, 0.0, -jnp.inf)

    @pl.when(jnp.logical_not(any_ties))
    def _():
        n_diag = TILE // CHUNK

        def body(c, carry):
            bias_ref[c] = jnp.where(key_ref[c] >= thr, 0.0, -jnp.inf)
            return carry
        lax.fori_loop(0, nch - n_diag, body, 0)
        for j in range(n_diag):
            c = nch - n_diag + j
            write_bias(c, key_ref[c] >= thr)

    @pl.when(any_ties)
    def _():
        lower_tri = (lax.broadcasted_iota(i32, (CHUNK, CHUNK), 1)
                     <= lax.broadcasted_iota(i32, (CHUNK, CHUNK), 0))
        lower_tri = jnp.where(lower_tri, 1.0, 0.0).astype(bf16)

        def body(c, seen):
            key = key_ref[c]
            tie = key == thr
            tie_f = jnp.where(tie, 1.0, 0.0)
            rank = jnp.dot(lower_tri, tie_f.astype(bf16), preferred_element_type=f32) + seen
            write_bias(c, (key > thr) | (tie & (rank <= ties_wanted)))
            return seen + jnp.sum(tie_f, axis=0, keepdims=True)
        lax.fori_loop(0, nch, body, jnp.zeros((1, TILE), f32))

    m_ref[...] = jnp.full((N_HEADS, TILE), NEG_BIG, f32)
    acc_ref[...] = jnp.zeros((N_HEADS, V_ROWS, TILE), f32)

    def attn_logits(c, buf_ref):
        kc = k_ref[c]
        bias = bias_ref[c]
        col_max = []
        for hd in range(N_HEADS):
            lt = jnp.dot(kc, qt_ref[hd * HEAD_DIM:(hd + 1) * HEAD_DIM, :],
                         preferred_element_type=f32) + bias
            buf_ref[hd] = lt
            col_max.append(jnp.max(lt, axis=0, keepdims=True))
        return tuple(col_max)

    def attn_accumulate(c, buf_ref, col_max):
        vc = vt_ref[c]
        for hd in range(N_HEADS):
            m_old = m_ref[hd:hd + 1, :]
            m_new = jnp.maximum(m_old, col_max[hd])
            p = jnp.exp2(buf_ref[hd] - m_new).astype(bf16)
            alpha = jnp.exp2(m_old - m_new)
            acc_ref[hd] = acc_ref[hd] * alpha + jnp.dot(vc, p, preferred_element_type=f32)
            m_ref[hd:hd + 1, :] = m_new

    def attn_pair(p, max_a):
        c = 2 * p
        max_b = attn_logits(c + 1, ltb_ref)
        attn_accumulate(c, lta_ref, max_a)
        max_a = attn_logits(c + 2, lta_ref)
        attn_accumulate(c + 1, ltb_ref, max_b)
        return max_a

    n_pairs = (nch - 1) // 2
    max_a = lax.fori_loop(0, n_pairs, attn_pair, attn_logits(0, lta_ref))
    c_last = 2 * n_pairs

    @pl.when(c_last == nch - 1)
    def _():
        attn_accumulate(c_last, lta_ref, max_a)

    @pl.when(c_last != nch - 1)
    def _():
        max_b = attn_logits(c_last + 1, ltb_ref)
        attn_accumulate(c_last, lta_ref, max_a)
        attn_accumulate(c_last + 1, ltb_ref, max_b)
    for hd in range(N_HEADS):
        a = acc_ref[hd]
        attn_ref[hd * HEAD_DIM:(hd + 1) * HEAD_DIM, :] = a[0:HEAD_DIM] / a[HEAD_DIM:HEAD_DIM + 1]
    attn = attn_ref[...].T

    def proj_cols(start, stop):
        return jnp.concatenate([pn_ref[s] for s in range(start // SLAB, stop // SLAB)], axis=1)

    u_a = proj_cols(N_UA, N_ZA)

    @pl.when(i == 0)
    def _():
        ext_ref[0:MAX_WINDOW, :] = jnp.zeros((MAX_WINDOW, POOL_WIDTH), f32)

    @pl.when(i > 0)
    def _():
        ext_ref[0:MAX_WINDOW, :] = ext_ref[TILE:TILE + MAX_WINDOW, :]

    ext_ref[MAX_WINDOW:MAX_WINDOW + TILE, :] = u_a
    t_pos = q0 + lax.broadcasted_iota(i32, (TILE, 1), 0)
    mixed = []
    for g, w in enumerate(POOL_WINDOWS):
        cols = slice(g * POOL_GROUP_DIM, (g + 1) * POOL_GROUP_DIM)
        win = ext_ref[MAX_WINDOW:MAX_WINDOW + TILE, cols]
        for j in range(1, w):
            win = win + ext_ref[MAX_WINDOW - j:MAX_WINDOW - j + TILE, cols]
        count = jnp.minimum(t_pos + 1, w).astype(f32)
        pooled = win / count - u_a[:, cols]
        mixed.append(jnp.dot(pooled.astype(bf16), wpool_ref[g], preferred_element_type=f32))
    mixed = jnp.concatenate(mixed, axis=1) * pscale_ref[...]
    z_a = proj_cols(N_ZA, N_ZB)
    y_a = mixed * (z_a * jax.nn.sigmoid(z_a))
    a_out = jnp.dot(y_a.astype(bf16), wa_ref[...], preferred_element_type=f32)

    z_b = proj_cols(N_ZB, N_GA)
    y_b = attn * (z_b * jax.nn.sigmoid(z_b))
    b_out = jnp.dot(y_b.astype(bf16), wb_ref[...], preferred_element_type=f32)
    g_a = proj_cols(N_GA, N_GB)
    g_b = proj_cols(N_GB, N_END)
    merged = jax.nn.sigmoid(g_a) * a_out + jax.nn.sigmoid(g_b) * b_out
    y = jnp.dot(merged.astype(bf16), wo_ref[...], preferred_element_type=f32)
    out_ref[0] = x + gate * y


def _const_spec(shape):
    zeros = (0,) * len(shape)
    return pl.BlockSpec(shape, lambda b, i: zeros, pipeline_mode=pl.Buffered(1))


def _layer(x, mod3, pos3, gain, invf, w_n, w_t, w_ti, w_pool, pscale, qg, kg, w_a, w_b, w_o):
    batch, seq_len, d = x.shape
    assert d == D_MODEL and seq_len % TILE == 0 and TILE % CHUNK == 0
    assert seq_len // 32 <= 256, "bf16 partial counts must stay exactly representable"
    n_chunks = seq_len // CHUNK
    top_k = min(TOPK_MAX, seq_len // 4)
    bf16, f32, i32 = jnp.bfloat16, jnp.float32, jnp.int32
    kernel = functools.partial(_layer_kernel, seq_len=seq_len, top_k=top_k)
    return pl.pallas_call(
        kernel,
        grid=(batch, seq_len // TILE),
        in_specs=[
            pl.BlockSpec((1, TILE, D_MODEL), lambda b, i: (b, i, 0)),
            pl.BlockSpec((1, 3, D_MODEL), lambda b, i: (b, 0, 0)),
            pl.BlockSpec((1, 1, TILE), lambda b, i: (b, 0, i)),
            _const_spec(gain.shape), _const_spec(invf.shape), _const_spec(w_n.shape),
            _const_spec(w_t.shape), _const_spec(w_ti.shape), _const_spec(w_pool.shape), _const_spec(pscale.shape),
            _const_spec(qg.shape), _const_spec(kg.shape), _const_spec(w_a.shape),
            _const_spec(w_b.shape), _const_spec(w_o.shape),
        ],
        out_specs=pl.BlockSpec((1, TILE, D_MODEL), lambda b, i: (b, i, 0)),
        out_shape=jax.ShapeDtypeStruct(x.shape, x.dtype),
        scratch_shapes=[
            pltpu.VMEM((TILE + MAX_WINDOW, POOL_WIDTH), f32),
            pltpu.VMEM((n_chunks, CHUNK, HEAD_DIM), bf16),
            pltpu.VMEM((n_chunks, CHUNK, IDX_K), bf16),
            pltpu.VMEM((n_chunks, V_ROWS, CHUNK), bf16),
            pltpu.VMEM((ATTN_WIDTH, TILE), bf16),
            pltpu.VMEM((IDX_HEADS * IDX_K, TILE), bf16),
            pltpu.VMEM((IDX_HEADS, TILE), f32),
            pltpu.VMEM((n_chunks, CHUNK, TILE), i32),
            pltpu.VMEM((n_chunks, CHUNK, TILE), f32),
            pltpu.VMEM((N_HEADS, TILE), f32),
            pltpu.VMEM((N_HEADS, V_ROWS, TILE), f32),
            pltpu.VMEM((ATTN_WIDTH, TILE), f32),
            pltpu.VMEM((N_HEADS, CHUNK, TILE), f32),
            pltpu.VMEM((N_HEADS, CHUNK, TILE), f32),
            pltpu.VMEM((n_chunks, CHUNK, TILE), jnp.int16),
            pltpu.VMEM((TILE, D_MODEL), bf16),
            pltpu.VMEM((N_SLABS, TILE, SLAB), f32),
            pltpu.VMEM((1, TILE), i32),
            pltpu.VMEM((2, TILE), f32),
        ],
        compiler_params=pltpu.CompilerParams(
            dimension_semantics=("arbitrary", "arbitrary"),
            vmem_limit_bytes=VMEM_LIMIT_BYTES),
        name="hybrid_layer",
    )(x, mod3, pos3, gain, invf, w_n, w_t, w_ti, w_pool, pscale, qg, kg, w_a, w_b, w_o)


def _modulation(c, w_ada, b_ada):
    batch, d = c.shape
    n_out = w_ada.shape[1]
    blk = D_MODEL
    return pl.pallas_call(
        _mod_kernel,
        grid=(n_out // blk,),
        in_specs=[pl.BlockSpec((batch, d), lambda j: (0, 0)),
                  pl.BlockSpec((d, blk), lambda j: (0, j)),
                  pl.BlockSpec((1, blk), lambda j: (0, j))],
        out_specs=pl.BlockSpec((batch, blk), lambda j: (0, j)),
        out_shape=jax.ShapeDtypeStruct((batch, n_out), jnp.float32),
        name="adaln_mod",
    )(c, w_ada, b_ada.reshape(1, n_out))


def kernel(x, c, positions, w_ada, b_ada, norm_gain, w_in, w_pool_group, pool_scale,
           q_norm_gain, k_norm_gain, w_branch_a, w_branch_b, w_out):
    batch, seq_len, _ = x.shape
    bf16 = jnp.bfloat16
    inv_freq = ROPE_THETA ** (-jnp.arange(0, ROT_DIM, 2, dtype=jnp.float32) / ROT_DIM)
    invf = inv_freq.reshape(ROT_HALF, 1)
    pos3 = positions.reshape(batch, 1, seq_len)
    bounds = [0]
    for s in IN_SIZES:
        bounds.append(bounds[-1] + s)
    depth = w_in.shape[0]
    for layer in range(depth):
        seg = [w_in[layer][:, bounds[j]:bounds[j + 1]] for j in range(len(IN_SIZES))]
        u_a, z_a, q, k, v, z_b, q_idx, k_idx, w_idx, g_a, g_b = seg
        w_n = jnp.concatenate([u_a, z_a, z_b, g_a, g_b], axis=1).astype(bf16)
        w_n = w_n.reshape(D_MODEL, N_SLABS, SLAB).transpose(1, 0, 2)
        w_t = jnp.concatenate([q, k, v], axis=1).T.astype(bf16)
        pad = jnp.zeros((D_MODEL, I_END - I_W - IDX_HEADS), jnp.float32)
        w_ti = jnp.concatenate(_split3(jnp.concatenate([q_idx, k_idx, w_idx, pad], axis=1).T),
                               axis=0).astype(bf16)
        mod = _modulation(c, w_ada[layer], b_ada[layer])
        x = _layer(
            x, mod.reshape(batch, 3, D_MODEL), pos3, norm_gain[layer].reshape(1, D_MODEL), invf,
            w_n, w_t, w_ti, w_pool_group[layer].astype(bf16), pool_scale[layer].reshape(1, POOL_WIDTH),
            q_norm_gain[layer].reshape(HEAD_DIM, 1), k_norm_gain[layer].reshape(HEAD_DIM, 1),
            w_branch_a[layer].astype(bf16), w_branch_b[layer].astype(bf16), w_out[layer].astype(bf16))
    return x
```
